```python
import jax, jax.numpy as jnp
from jax import lax
import numpy as np

D_MODEL = 4096
BATCH = 1
SEQ = 16384
DEPTH = 2

SB_HEAD_DIM = 128
SB_WIDTH = D_MODEL // 4
SB_HEADS = SB_WIDTH // SB_HEAD_DIM
Q_BLOCK = 128
CONV_CHANNELS = D_MODEL // 4
CONV_WIDTH = 31
MEM_TOKENS = 256
MEM_HEADS = 4
MEM_WIDTH = D_MODEL // 4
MEM_HEAD_DIM = MEM_WIDTH // MEM_HEADS
N_BRANCHES = 3
FFN_DIM = 2 * D_MODEL
FFN_CONV_WIDTH = 3
NORM_EPS = 1e-6

IN_SPLITS = [
    SB_WIDTH,
    2 * SB_WIDTH,
    3 * SB_WIDTH,
    3 * SB_WIDTH + 2 * CONV_CHANNELS,
    3 * SB_WIDTH + 2 * CONV_CHANNELS + MEM_WIDTH,
]
IN_WIDTH = IN_SPLITS[-1] + N_BRANCHES * D_MODEL

kernel_name = 'hybrid_stickbreak_conformer_memory_block'


def _rmsnorm(x, g):
    xf = x.astype(jnp.float32)
    y = xf * lax.rsqrt(jnp.mean(xf * xf, axis=-1, keepdims=True) + NORM_EPS)
    return (y * g.astype(jnp.float32)).astype(x.dtype)


def _layernorm(x, g, b):
    xf = x.astype(jnp.float32)
    mu = jnp.mean(xf, axis=-1, keepdims=True)
    xc = xf - mu
    var = jnp.mean(xc * xc, axis=-1, keepdims=True)
    y = xc * lax.rsqrt(var + NORM_EPS) * g.astype(jnp.float32) + b.astype(jnp.float32)
    return y.astype(x.dtype)


def _causal_dwconv(x, w, b):
    k_width, ch = w.shape
    y = lax.conv_general_dilated(
        x, w[:, None, :], window_strides=(1,), padding=[(k_width - 1, 0)],
        dimension_numbers=('NWC', 'WIO', 'NWC'), feature_group_count=ch)
    return y + b


def _stick_breaking_attention(q, k, v):
    b, s, h, dh = q.shape
    scale = dh ** -0.5
    outs = []
    for i in range(s // Q_BLOCK):
        lo = i * Q_BLOCK
        hi = lo + Q_BLOCK
        z = jnp.einsum('bqhd,bkhd->bhqk', q[:, lo:hi], k[:, :hi]).astype(jnp.float32) * scale
        mask = jnp.arange(hi)[None, :] < (lo + jnp.arange(Q_BLOCK))[:, None]
        l = jnp.where(mask, jax.nn.log_sigmoid(-z), 0.0)
        log_a = jnp.where(mask, z + lax.cumsum(l, axis=3, reverse=True), -jnp.inf)
        a = jnp.exp(log_a).astype(v.dtype)
        outs.append(jnp.einsum('bhqk,bkhd->bqhd', a, v[:, :hi]))
    return jnp.concatenate(outs, axis=1).reshape(b, s, h * dh)


def _memory_cross_attention(mq, mem_n, wk, wv):
    b, s, _ = mq.shape
    m = mem_n.shape[1]
    qh = mq.reshape(b, s, MEM_HEADS, MEM_HEAD_DIM)
    kh = (mem_n @ wk).reshape(b, m, MEM_HEADS, MEM_HEAD_DIM)
    vh = (mem_n @ wv).reshape(b, m, MEM_HEADS, MEM_HEAD_DIM)
    scores = jnp.einsum('bshd,bmhd->bhsm', qh, kh).astype(jnp.float32) * (MEM_HEAD_DIM ** -0.5)
    p = jax.nn.softmax(scores, axis=-1).astype(vh.dtype)
    return jnp.einsum('bhsm,bmhd->bshd', p, vh).reshape(b, s, MEM_WIDTH)


def setup_inputs(seed: int = 0) -> dict:
    key = jax.random.key(seed)
    ks = jax.random.split(key, 24)
    L = DEPTH

    def normal(k, shape, fan_in):
        return jax.random.normal(k, shape, jnp.float32) * (fan_in ** -0.5)

    def gain(k, shape):
        return 1.0 + 0.02 * jax.random.normal(k, shape, jnp.float32)

    def bias(k, shape):
        return 0.02 * jax.random.normal(k, shape, jnp.float32)

    return {
        'x': jax.random.normal(ks[0], (BATCH, SEQ, D_MODEL), jnp.float32),
        'mem': jax.random.normal(ks[1], (BATCH, MEM_TOKENS, D_MODEL), jnp.float32),
        'norm_mix_pre': gain(ks[2], (L, D_MODEL)),
        'w_in': normal(ks[3], (L, D_MODEL, IN_WIDTH), D_MODEL),
        'sb_wo': normal(ks[4], (L, SB_WIDTH, D_MODEL), SB_WIDTH),
        'cv_dw': normal(ks[5], (L, CONV_WIDTH, CONV_CHANNELS), CONV_WIDTH),
        'cv_dw_b': bias(ks[6], (L, CONV_CHANNELS)),
        'cv_ln_g': gain(ks[7], (L, CONV_CHANNELS)),
        'cv_ln_b': bias(ks[8], (L, CONV_CHANNELS)),
        'cv_wo': normal(ks[9], (L, CONV_CHANNELS, D_MODEL), CONV_CHANNELS),
        'mem_norm': gain(ks[10], (L, D_MODEL)),
        'mem_wk': normal(ks[11], (L, D_MODEL, MEM_WIDTH), D_MODEL),
        'mem_wv': normal(ks[12], (L, D_MODEL, MEM_WIDTH), D_MODEL),
        'mem_wo': normal(ks[13], (L, MEM_WIDTH, D_MODEL), MEM_WIDTH),
        'w_out': normal(ks[14], (L, D_MODEL, D_MODEL), D_MODEL),
        'norm_mix_post': gain(ks[15], (L, D_MODEL)),
        'norm_ffn_pre': gain(ks[16], (L, D_MODEL)),
        'ffn_up': normal(ks[17], (L, D_MODEL, 2 * FFN_DIM), D_MODEL),
        'ffn_dw': normal(ks[18], (L, FFN_CONV_WIDTH, 2 * FFN_DIM), FFN_CONV_WIDTH),
        'ffn_dw_b': bias(ks[19], (L, 2 * FFN_DIM)),
        'ffn_down': normal(ks[20], (L, FFN_DIM, D_MODEL), FFN_DIM),
        'norm_ffn_post': gain(ks[21], (L, D_MODEL)),
    }


def reference(x, mem, norm_mix_pre, w_in, sb_wo, cv_dw, cv_dw_b, cv_ln_g, cv_ln_b, cv_wo,
              mem_norm, mem_wk, mem_wv, mem_wo, w_out, norm_mix_post, norm_ffn_pre,
              ffn_up, ffn_dw, ffn_dw_b, ffn_down, norm_ffn_post):
    b, s, _ = x.shape
    for l in range(DEPTH):
        h = _rmsnorm(x, norm_mix_pre[l])
        proj = h @ w_in[l]
        q, k, v, glu_in, mq, gates = jnp.split(proj, IN_SPLITS, axis=-1)

        q = q.reshape(b, s, SB_HEADS, SB_HEAD_DIM)
        k = k.reshape(b, s, SB_HEADS, SB_HEAD_DIM)
        v = v.reshape(b, s, SB_HEADS, SB_HEAD_DIM)
        y_sb = _stick_breaking_attention(q, k, v) @ sb_wo[l]

        glu_a, glu_b = jnp.split(glu_in, 2, axis=-1)
        c = glu_a * jax.nn.sigmoid(glu_b)
        c = _causal_dwconv(c, cv_dw[l], cv_dw_b[l])
        c = jax.nn.silu(_layernorm(c, cv_ln_g[l], cv_ln_b[l]))
        y_cv = c @ cv_wo[l]

        mem_n = _rmsnorm(mem, mem_norm[l])
        y_mem = _memory_cross_attention(mq, mem_n, mem_wk[l], mem_wv[l]) @ mem_wo[l]

        g_sb, g_cv, g_mem = jnp.split(gates, N_BRANCHES, axis=-1)
        merged = (jax.nn.sigmoid(g_sb) * y_sb + jax.nn.sigmoid(g_cv) * y_cv
                  + jax.nn.sigmoid(g_mem) * y_mem)
        x = x + _rmsnorm(merged @ w_out[l], norm_mix_post[l])

        h2 = _rmsnorm(x, norm_ffn_pre[l])
        u = _causal_dwconv(h2 @ ffn_up[l], ffn_dw[l], ffn_dw_b[l])
        u_gate, u_val = jnp.split(u, 2, axis=-1)
        f = jax.nn.gelu(u_gate, approximate=True) * u_val
        x = x + _rmsnorm(f @ ffn_down[l], norm_ffn_post[l])
    return x
```

```python
import functools

import jax
import jax.numpy as jnp
from jax import lax
from jax.experimental import pallas as pl
from jax.experimental.pallas import tpu as pltpu

F32 = jnp.float32
BF16 = jnp.bfloat16

NORM_EPS = 1e-6
SB_HEAD_DIM = 128
MEM_HEADS = 4

V7X_VMEM_BYTES = 64 * 1024 * 1024
VMEM_LIMIT_CAP = V7X_VMEM_BYTES - 6 * 1024 * 1024
MATMUL_BLOCK_BUDGET = 44 * 1024 * 1024
BF16_SUBLANES = 16
F32_SUBLANES = 8
F32_EXP_UNDERFLOW = -104.0


def _tile(n, pref):
    t = min(n, pref)
    while n % t:
        t //= 2
    return t


def _params(n_axes, vmem_bytes):
    return pltpu.CompilerParams(
        dimension_semantics=("arbitrary",) * n_axes,
        vmem_limit_bytes=int(min(max(vmem_bytes, 16 * 1024 * 1024), VMEM_LIMIT_CAP)))


def _nbytes(shape, dtype):
    n = jnp.dtype(dtype).itemsize
    for s in shape:
        n *= s
    return n


def _dot(a, b):
    return jnp.dot(a, b, preferred_element_type=F32)


def _dot_nt(a, b):
    return lax.dot_general(a, b, (((1,), (1,)), ((), ())), preferred_element_type=F32)


def _rms(x, g):
    return x * lax.rsqrt(jnp.mean(x * x, axis=-1, keepdims=True) + NORM_EPS) * g


def _rmsnorm_kernel(x_ref, g_ref, o_ref):
    o_ref[...] = _rms(x_ref[...], g_ref[...]).astype(o_ref.dtype)


def _rmsnorm(x, g, tm=512):
    m, d = x.shape
    tm = _tile(m, tm)
    blocks = _nbytes((tm, d), F32) + _nbytes((tm, d), BF16)
    return pl.pallas_call(
        _rmsnorm_kernel,
        out_shape=jax.ShapeDtypeStruct((m, d), BF16),
        grid=(m // tm,),
        in_specs=[pl.BlockSpec((tm, d), lambda i: (i, 0)),
                  pl.BlockSpec((1, d), lambda i: (0, 0))],
        out_specs=pl.BlockSpec((tm, d), lambda i: (i, 0)),
        compiler_params=_params(1, 2 * blocks + 2 * _nbytes((tm, d), F32)),
        name="rmsnorm",
    )(x, g.reshape(1, d))


def _mm_kernel(a_ref, b_ref, o_ref):
    o_ref[...] = _dot(a_ref[...], b_ref[...]).astype(o_ref.dtype)


def _matmul_tiles(m, k, n, out_dtype):
    tm, tn = _tile(m, 1024), _tile(n, 1024)

    def need(tm, tn):
        blocks = _nbytes((tm, k), BF16) + _nbytes((k, tn), BF16) + _nbytes((tm, tn), out_dtype)
        return 2 * blocks + _nbytes((tm, tn), F32)

    while need(tm, tn) > MATMUL_BLOCK_BUDGET and tn > 256:
        tn //= 2
    while need(tm, tn) > MATMUL_BLOCK_BUDGET and tm > 256:
        tm //= 2
    return tm, tn, need(tm, tn)


def _matmul(a, b, out_dtype, name):
    m, k = a.shape
    n = b.shape[1]
    tm, tn, need = _matmul_tiles(m, k, n, out_dtype)
    return pl.pallas_call(
        _mm_kernel,
        out_shape=jax.ShapeDtypeStruct((m, n), out_dtype),
        grid=(m // tm, n // tn),
        in_specs=[pl.BlockSpec((tm, k), lambda i, j: (i, 0)),
                  pl.BlockSpec((k, tn), lambda i, j: (0, j))],
        out_specs=pl.BlockSpec((tm, tn), lambda i, j: (i, j)),
        compiler_params=_params(2, need + 4 * 1024 * 1024),
        name=name,
    )(a, b)


def _glu_kernel(a_ref, wa_ref, wb_ref, o_ref):
    a = a_ref[...]
    o_ref[...] = _dot(a, wa_ref[...]) * jax.nn.sigmoid(_dot(a, wb_ref[...]))


def _glu_proj(a, w, tm=1024, tn=512):
    m, k = a.shape
    c = w.shape[1] // 2
    tm, tn = _tile(m, tm), _tile(c, tn)
    nj = c // tn
    blocks = _nbytes((tm, k), BF16) + 2 * _nbytes((k, tn), BF16) + _nbytes((tm, tn), F32)
    return pl.pallas_call(
        _glu_kernel,
        out_shape=jax.ShapeDtypeStruct((m, c), F32),
        grid=(m // tm, nj),
        in_specs=[pl.BlockSpec((tm, k), lambda i, j: (i, 0)),
                  pl.BlockSpec((k, tn), lambda i, j: (0, j)),
                  pl.BlockSpec((k, tn), lambda i, j: (0, j + nj))],
        out_specs=pl.BlockSpec((tm, tn), lambda i, j: (i, j)),
        compiler_params=_params(2, 2 * blocks + 8 * _nbytes((tm, tn), F32)),
        name="glu_proj",
    )(a, w, w)


def _sb_tile(q, kblk, vblk, tri, carry, scale, mask):
    z = _dot_nt(q, kblk) * scale
    l = -(jnp.maximum(z, 0.0) + jnp.log(1.0 + jnp.exp(-jnp.abs(z))))
    if mask is not None:
        l = jnp.where(mask, l, 0.0)
    l_hi = l.astype(BF16)
    l_lo = (l - l_hi.astype(F32)).astype(BF16)
    cum = _dot(l_hi, tri) + _dot(l_lo, tri)
    a = jnp.exp(z + cum + carry)
    if mask is not None:
        a = jnp.where(mask, a, 0.0)
    pv = _dot(a.astype(BF16), vblk)
    return pv, carry + cum[:, 0:1]


def _sb_attn_kernel(q_ref, k_ref, v_ref, o_ref, *, tq, scale):
    i = pl.program_id(1)
    q = q_ref[...]
    rows = lax.broadcasted_iota(jnp.int32, (tq, tq), 0)
    cols = lax.broadcasted_iota(jnp.int32, (tq, tq), 1)
    tri = (rows >= cols).astype(BF16)
    causal = cols < rows

    start = pl.multiple_of(i * tq, tq)
    acc, carry = _sb_tile(q, k_ref[pl.ds(start, tq), :], v_ref[pl.ds(start, tq), :],
                          tri, jnp.zeros((tq, 1), F32), scale, causal)

    def body(step, state):
        acc, carry = state
        kb = pl.multiple_of((i - 1 - step) * tq, tq)
        pv, carry = _sb_tile(q, k_ref[pl.ds(kb, tq), :], v_ref[pl.ds(kb, tq), :],
                             tri, carry, scale, None)
        return acc + pv, carry

    acc, _ = lax.fori_loop(0, i, body, (acc, carry))
    o_ref[...] = acc.astype(o_ref.dtype)


def _sb_attention(qkv, heads, tq=256):
    s = qkv.shape[0]
    dh = SB_HEAD_DIM
    tq = _tile(s, tq)
    kv_bytes = 2 * _nbytes((s, dh), BF16)
    blocks = 2 * _nbytes((tq, dh), BF16) + kv_bytes
    return pl.pallas_call(
        functools.partial(_sb_attn_kernel, tq=tq, scale=dh ** -0.5),
        out_shape=jax.ShapeDtypeStruct((s, heads * dh), BF16),
        grid=(heads, s // tq),
        in_specs=[pl.BlockSpec((tq, dh), lambda h, i: (i, h)),
                  pl.BlockSpec((s, dh), lambda h, i: (0, heads + h)),
                  pl.BlockSpec((s, dh), lambda h, i: (0, 2 * heads + h))],
        out_specs=pl.BlockSpec((tq, dh), lambda h, i: (i, h)),
        compiler_params=_params(2, 2 * blocks + 16 * _nbytes((tq, tq), F32)),
        name="sb_attention",
    )(qkv, qkv, qkv)


def _conv_module_kernel(halo_ref, c_ref, dw_ref, dwb_ref, g_ref, b_ref, o_ref,
                        ext_ref, y_ref, *, tm, halo, lanes):
    i = pl.program_id(0)
    kw = dw_ref.shape[0]
    ch = c_ref.shape[1]
    ext_ref[0:halo, :] = jnp.where(i > 0, halo_ref[...], 0.0)
    ext_ref[halo:halo + tm, :] = c_ref[...]
    for c0 in range(0, ch, lanes):
        acc = None
        for t in range(kw):
            off = halo - (kw - 1) + t
            term = dw_ref[t:t + 1, c0:c0 + lanes] * ext_ref[off:off + tm, c0:c0 + lanes]
            acc = term if acc is None else acc + term
        y_ref[:, c0:c0 + lanes] = acc + dwb_ref[:, c0:c0 + lanes]
    y = y_ref[...]
    mu = jnp.mean(y, axis=-1, keepdims=True)
    yc = y - mu
    var = jnp.mean(yc * yc, axis=-1, keepdims=True)
    yn = yc * lax.rsqrt(var + NORM_EPS) * g_ref[...] + b_ref[...]
    o_ref[...] = (yn * jax.nn.sigmoid(yn)).astype(o_ref.dtype)


def _conv_module(c, dw, dwb, ln_g, ln_b, tm=128):
    s, ch = c.shape
    kw = dw.shape[0]
    halo = -(-(kw - 1) // F32_SUBLANES) * F32_SUBLANES
    tm = _tile(s, tm)
    assert tm % halo == 0 or s == tm
    lanes = _tile(ch, 256)
    hb = tm // halo
    vec = pl.BlockSpec((1, ch), lambda i: (0, 0))
    blocks = (_nbytes((halo + tm, ch), F32) + _nbytes((kw, ch), F32) + _nbytes((tm, ch), BF16))
    scratch = _nbytes((halo + 2 * tm, ch), F32)
    return pl.pallas_call(
        functools.partial(_conv_module_kernel, tm=tm, halo=halo, lanes=lanes),
        out_shape=jax.ShapeDtypeStruct((s, ch), BF16),
        grid=(s // tm,),
        in_specs=[pl.BlockSpec((halo, ch), lambda i: (jnp.maximum(i * hb - 1, 0), 0)),
                  pl.BlockSpec((tm, ch), lambda i: (i, 0)),
                  pl.BlockSpec((kw, ch), lambda i: (0, 0)),
                  vec, vec, vec],
        out_specs=pl.BlockSpec((tm, ch), lambda i: (i, 0)),
        scratch_shapes=[pltpu.VMEM((halo + tm, ch), F32), pltpu.VMEM((tm, ch), F32)],
        compiler_params=_params(1, 2 * blocks + scratch + 6 * _nbytes((tm, ch), F32)),
        name="conv_module",
    )(c, c, dw, dwb.reshape(1, ch), ln_g.reshape(1, ch), ln_b.reshape(1, ch))


def _mem_attn_kernel(q_ref, k_ref, v_ref, o_ref, *, heads, scale):
    dh = q_ref.shape[1] // heads
    for h in range(heads):
        sl = slice(h * dh, (h + 1) * dh)
        s = _dot_nt(q_ref[:, sl], k_ref[:, sl]) * scale
        e = jnp.exp(s - jnp.max(s, axis=-1, keepdims=True))
        p = e / jnp.sum(e, axis=-1, keepdims=True)
        o_ref[:, sl] = _dot(p.astype(BF16), v_ref[:, sl]).astype(o_ref.dtype)


def _mem_attention(qsrc, q_col_block, kh, vh, tm=512):
    s = qsrc.shape[0]
    mtok, w = kh.shape
    tm = _tile(s, tm)
    blocks = 2 * _nbytes((tm, w), BF16) + 2 * _nbytes((mtok, w), BF16)
    return pl.pallas_call(
        functools.partial(_mem_attn_kernel, heads=MEM_HEADS, scale=(w // MEM_HEADS) ** -0.5),
        out_shape=jax.ShapeDtypeStruct((s, w), BF16),
        grid=(s // tm,),
        in_specs=[pl.BlockSpec((tm, w), lambda i: (i, q_col_block)),
                  pl.BlockSpec((mtok, w), lambda i: (0, 0)),
                  pl.BlockSpec((mtok, w), lambda i: (0, 0))],
        out_specs=pl.BlockSpec((tm, w), lambda i: (i, 0)),
        compiler_params=_params(1, 2 * blocks + 8 * _nbytes((tm, mtok), F32)),
        name="mem_attention",
    )(qsrc, kh, vh)


def _merge_kernel(h_ref, a_sb_ref, a_cv_ref, a_mem_ref, g_sb_ref, g_cv_ref, g_mem_ref,
                  w_sb_ref, w_cv_ref, w_mem_ref, o_ref):
    h = h_ref[...]

    def branch(a_ref, w_ref, g_ref):
        return jax.nn.sigmoid(_dot(h, g_ref[...])) * _dot(a_ref[...], w_ref[...])

    merged = (branch(a_sb_ref, w_sb_ref, g_sb_ref) + branch(a_cv_ref, w_cv_ref, g_cv_ref)
              + branch(a_mem_ref, w_mem_ref, g_mem_ref))
    o_ref[...] = merged.astype(o_ref.dtype)


def _gated_merge(h, a_sb, a_cv, a_mem, w_gates, w_sb, w_cv, w_mem, tm=1024, tn=256):
    m, d = h.shape
    tm, tn = _tile(m, tm), _tile(d, tn)
    nj = d // tn
    row = lambda a: pl.BlockSpec((tm, a.shape[1]), lambda i, j: (i, 0))
    gate = lambda b: pl.BlockSpec((d, tn), lambda i, j: (0, j + b * nj))
    col = lambda w: pl.BlockSpec((w.shape[0], tn), lambda i, j: (0, j))
    blocks = (_nbytes((tm, d), BF16) + 3 * _nbytes((d, tn), BF16) + _nbytes((tm, tn), BF16)
              + sum(_nbytes((tm, a.shape[1]), BF16) + _nbytes((a.shape[1], tn), BF16)
                    for a in (a_sb, a_cv, a_mem)))
    return pl.pallas_call(
        _merge_kernel,
        out_shape=jax.ShapeDtypeStruct((m, d), BF16),
        grid=(m // tm, nj),
        in_specs=[row(h), row(a_sb), row(a_cv), row(a_mem), gate(0), gate(1), gate(2),
                  col(w_sb), col(w_cv), col(w_mem)],
        out_specs=pl.BlockSpec((tm, tn), lambda i, j: (i, j)),
        compiler_params=_params(2, 2 * blocks + 12 * _nbytes((tm, tn), F32)),
        name="gated_merge",
    )(h, a_sb, a_cv, a_mem, w_gates, w_gates, w_gates, w_sb, w_cv, w_mem)


def _residual_norm_kernel(y_ref, x_ref, g_ref, gn_ref, xo_ref, ho_ref):
    xn = x_ref[...] + _rms(y_ref[...], g_ref[...])
    xo_ref[...] = xn
    ho_ref[...] = _rms(xn, gn_ref[...]).astype(ho_ref.dtype)


def _residual_norm_last_kernel(y_ref, x_ref, g_ref, xo_ref):
    xo_ref[...] = x_ref[...] + _rms(y_ref[...], g_ref[...])


def _residual_norm(y, x, g, g_next, tm=256):
    m, d = x.shape
    tm = _tile(m, tm)
    blk = pl.BlockSpec((tm, d), lambda i: (i, 0))
    vec = pl.BlockSpec((1, d), lambda i: (0, 0))
    f32_blk = _nbytes((tm, d), F32)
    if g_next is None:
        return pl.pallas_call(
            _residual_norm_last_kernel,
            out_shape=jax.ShapeDtypeStruct((m, d), F32),
            grid=(m // tm,), in_specs=[blk, blk, vec], out_specs=blk,
            compiler_params=_params(1, 10 * f32_blk),
            name="residual_norm_last",
        )(y, x, g.reshape(1, d)), None
    return pl.pallas_call(
        _residual_norm_kernel,
        out_shape=(jax.ShapeDtypeStruct((m, d), F32), jax.ShapeDtypeStruct((m, d), BF16)),
        grid=(m // tm,), in_specs=[blk, blk, vec, vec], out_specs=(blk, blk),
        compiler_params=_params(1, 12 * f32_blk),
        name="residual_norm",
    )(y, x, g.reshape(1, d), g_next.reshape(1, d))


def _ffn_up_kernel(halo_ref, h_ref, wg_ref, wv_ref, dwg_ref, dwv_ref, bg_ref, bv_ref, o_ref,
                   a_ref, p_ref, *, tm, halo):
    i = pl.program_id(0)

    @pl.when(pl.program_id(1) == 0)
    def _():
        a_ref[0:halo, :] = jnp.where(i > 0, halo_ref[...], jnp.zeros_like(halo_ref))
        a_ref[halo:halo + tm, :] = h_ref[...]

    a = a_ref[...]

    def conv(w_ref, dw_ref, b_ref):
        p_ref[...] = _dot(a, w_ref[...])
        kw = dw_ref.shape[0]
        u = None
        for t in range(kw):
            off = halo - (kw - 1) + t
            term = dw_ref[t:t + 1, :] * p_ref[off:off + tm, :]
            u = term if u is None else u + term
        return u + b_ref[...]

    u_gate = conv(wg_ref, dwg_ref, bg_ref)
    u_val = conv(wv_ref, dwv_ref, bv_ref)
    o_ref[...] = (jax.nn.gelu(u_gate, approximate=True) * u_val).astype(o_ref.dtype)


def _ffn_up(h, w_up, dw, dwb, tm=1024, tn=512):
    m, d = h.shape
    f = w_up.shape[1] // 2
    kw = dw.shape[0]
    halo = -(-(kw - 1) // BF16_SUBLANES) * BF16_SUBLANES
    tm, tn = _tile(m, tm), _tile(f, tn)
    assert tm % halo == 0
    nj, hb = f // tn, tm // halo
    wspec = lambda half: pl.BlockSpec((d, tn), lambda i, j: (0, j + half * nj))
    dwspec = lambda half: pl.BlockSpec((kw, tn), lambda i, j: (0, j + half * nj))
    bspec = lambda half: pl.BlockSpec((1, tn), lambda i, j: (0, j + half * nj))
    blocks = (_nbytes((halo + tm, d), BF16) + 2 * _nbytes((d, tn), BF16) + _nbytes((tm, tn), BF16))
    scratch = _nbytes((halo + tm, d), BF16) + _nbytes((halo + tm, tn), F32)
    dwb2 = dwb.reshape(1, 2 * f)
    return pl.pallas_call(
        functools.partial(_ffn_up_kernel, tm=tm, halo=halo),
        out_shape=jax.ShapeDtypeStruct((m, f), BF16),
        grid=(m // tm, nj),
        in_specs=[pl.BlockSpec((halo, d), lambda i, j: (jnp.maximum(i * hb - 1, 0), 0)),
                  pl.BlockSpec((tm, d), lambda i, j: (i, 0)),
                  wspec(0), wspec(1), dwspec(0), dwspec(1), bspec(0), bspec(1)],
        out_specs=pl.BlockSpec((tm, tn), lambda i, j: (i, j)),
        scratch_shapes=[pltpu.VMEM((halo + tm, d), BF16), pltpu.VMEM((halo + tm, tn), F32)],
        compiler_params=_params(2, 2 * blocks + scratch + 6 * _nbytes((tm, tn), F32)),
        name="ffn_up",
    )(h, h, w_up, w_up, dw, dw, dwb2, dwb2)


def _forward_one(x, mem, p):
    depth = p["w_in"].shape[0]
    d = x.shape[1]
    sbw = p["sb_wo"].shape[1]
    ch = p["cv_wo"].shape[1]
    mw = p["mem_wo"].shape[1]
    heads = sbw // SB_HEAD_DIM
    assert (3 * sbw) % mw == 0
    glu0, mq0, gate0 = 3 * sbw, 3 * sbw + 2 * ch, 3 * sbw + 2 * ch + mw

    h = _rmsnorm(x, p["norm_mix_pre"][0])
    for l in range(depth):
        w_in = p["w_in"][l]
        w_qkvm = jnp.concatenate([w_in[:, :glu0], w_in[:, mq0:gate0]], axis=1).astype(BF16)
        w_glu = w_in[:, glu0:mq0].astype(BF16)
        w_gates = w_in[:, gate0:].astype(BF16)

        qkvm = _matmul(h, w_qkvm, BF16, "proj_qkvm")
        c = _glu_proj(h, w_glu)

        a_sb = _sb_attention(qkvm, heads)
        a_cv = _conv_module(c, p["cv_dw"][l], p["cv_dw_b"][l], p["cv_ln_g"][l], p["cv_ln_b"][l])

        mem_n = _rmsnorm(mem, p["mem_norm"][l])
        kh = _matmul(mem_n, p["mem_wk"][l].astype(BF16), BF16, "mem_k")
        vh = _matmul(mem_n, p["mem_wv"][l].astype(BF16), BF16, "mem_v")
        a_mem = _mem_attention(qkvm, (3 * sbw) // mw, kh, vh)

        merged = _gated_merge(h, a_sb, a_cv, a_mem, w_gates, p["sb_wo"][l].astype(BF16),
                              p["cv_wo"][l].astype(BF16), p["mem_wo"][l].astype(BF16))
        y = _matmul(merged, p["w_out"][l].astype(BF16), F32, "proj_out")
        x, h2 = _residual_norm(y, x, p["norm_mix_post"][l], p["norm_ffn_pre"][l])

        f = _ffn_up(h2, p["ffn_up"][l].astype(BF16), p["ffn_dw"][l], p["ffn_dw_b"][l])
        y = _matmul(f, p["ffn_down"][l].astype(BF16), F32, "ffn_down")
        g_next = p["norm_mix_pre"][l + 1] if l + 1 < depth else None
        x, h = _residual_norm(y, x, p["norm_ffn_post"][l], g_next)
    return x


def kernel(x, mem, norm_mix_pre, w_in, sb_wo, cv_dw, cv_dw_b, cv_ln_g, cv_ln_b, cv_wo, mem_norm, mem_wk, mem_wv, mem_wo, w_out, norm_mix_post, norm_ffn_pre, ffn_up, ffn_dw, ffn_dw_b, ffn_down, norm_ffn_post):
    p = dict(norm_mix_pre=norm_mix_pre, w_in=w_in, sb_wo=sb_wo, cv_dw=cv_dw, cv_dw_b=cv_dw_b,
             cv_ln_g=cv_ln_g, cv_ln_b=cv_ln_b, cv_wo=cv_wo, mem_norm=mem_norm, mem_wk=mem_wk,
             mem_wv=mem_wv, mem_wo=mem_wo, w_out=w_out, norm_mix_post=norm_mix_post,
             norm_ffn_pre=norm_ffn_pre, ffn_up=ffn_up, ffn_dw=ffn_dw, ffn_dw_b=ffn_dw_b,
             ffn_down=ffn_down, norm_ffn_post=norm_ffn_post)
    return jnp.stack([_forward_one(x[b], mem[b], p) for b in range(x.shape[0])])
```

```python
import functools

import jax
import jax.numpy as jnp
from jax import lax
from jax.experimental import pallas as pl
from jax.experimental.pallas import tpu as pltpu

F32 = jnp.float32
BF16 = jnp.bfloat16

NORM_EPS = 1e-6
SB_HEAD_DIM = 128
MEM_HEADS = 4

V7X_VMEM_BYTES = 64 * 1024 * 1024
VMEM_LIMIT_CAP = V7X_VMEM_BYTES - 6 * 1024 * 1024
MATMUL_BLOCK_BUDGET = 44 * 1024 * 1024
BF16_SUBLANES = 16
F32_SUBLANES = 8
SB_SKIP_LOG_WEIGHT = -128.0


def _tile(n, pref):
    t = min(n, pref)
    while n % t:
        t //= 2
    return t


def _params(n_axes, vmem_bytes):
    return pltpu.CompilerParams(
        dimension_semantics=("arbitrary",) * n_axes,
        vmem_limit_bytes=int(min(max(vmem_bytes, 16 * 1024 * 1024), VMEM_LIMIT_CAP)))


def _nbytes(shape, dtype):
    n = jnp.dtype(dtype).itemsize
    for s in shape:
        n *= s
    return n


def _dot(a, b):
    return jnp.dot(a, b, preferred_element_type=F32)


def _dot_nt(a, b):
    return lax.dot_general(a, b, (((1,), (1,)), ((), ())), preferred_element_type=F32)


def _rms(x, g):
    return x * lax.rsqrt(jnp.mean(x * x, axis=-1, keepdims=True) + NORM_EPS) * g


def _rmsnorm_kernel(x_ref, g_ref, o_ref):
    o_ref[...] = _rms(x_ref[...], g_ref[...]).astype(o_ref.dtype)


def _rmsnorm(x, g, tm=512):
    m, d = x.shape
    tm = _tile(m, tm)
    blocks = _nbytes((tm, d), F32) + _nbytes((tm, d), BF16)
    return pl.pallas_call(
        _rmsnorm_kernel,
        out_shape=jax.ShapeDtypeStruct((m, d), BF16),
        grid=(m // tm,),
        in_specs=[pl.BlockSpec((tm, d), lambda i: (i, 0)),
                  pl.BlockSpec((1, d), lambda i: (0, 0))],
        out_specs=pl.BlockSpec((tm, d), lambda i: (i, 0)),
        compiler_params=_params(1, 2 * blocks + 2 * _nbytes((tm, d), F32)),
        name="rmsnorm",
    )(x, g.reshape(1, d))


def _mm_kernel(a_ref, b_ref, o_ref):
    o_ref[...] = _dot(a_ref[...], b_ref[...]).astype(o_ref.dtype)


def _matmul_tiles(m, k, n, out_dtype):
    tm, tn = _tile(m, 1024), _tile(n, 1024)

    def need(tm, tn):
        blocks = _nbytes((tm, k), BF16) + _nbytes((k, tn), BF16) + _nbytes((tm, tn), out_dtype)
        return 2 * blocks + _nbytes((tm, tn), F32)

    while need(tm, tn) > MATMUL_BLOCK_BUDGET and tn > 256:
        tn //= 2
    while need(tm, tn) > MATMUL_BLOCK_BUDGET and tm > 256:
        tm //= 2
    return tm, tn, need(tm, tn)


def _matmul(a, b, out_dtype, name, col_ranges=None):
    m, k = a.shape
    col_ranges = col_ranges or [(0, b.shape[1])]
    n = sum(w for _, w in col_ranges)
    tm, tn, need = _matmul_tiles(m, k, n, out_dtype)
    while any(c % tn or w % tn for c, w in col_ranges):
        tn //= 2

    def b_block(j):
        blk, first = j, 0
        for c0, w in col_ranges:
            blk = jnp.where(j >= first, j - first + c0 // tn, blk)
            first += w // tn
        return blk

    return pl.pallas_call(
        _mm_kernel,
        out_shape=jax.ShapeDtypeStruct((m, n), out_dtype),
        grid=(m // tm, n // tn),
        in_specs=[pl.BlockSpec((tm, k), lambda i, j: (i, 0)),
                  pl.BlockSpec((k, tn), lambda i, j: (0, b_block(j)))],
        out_specs=pl.BlockSpec((tm, tn), lambda i, j: (i, j)),
        compiler_params=_params(2, need + 4 * 1024 * 1024),
        name=name,
    )(a, b)


def _glu_kernel(a_ref, wa_ref, wb_ref, o_ref):
    a = a_ref[...]
    o_ref[...] = _dot(a, wa_ref[...]) * jax.nn.sigmoid(_dot(a, wb_ref[...]))


def _glu_proj(a, w, col0, c, tm=1024, tn=512):
    m, k = a.shape
    tm, tn = _tile(m, tm), _tile(c, tn)
    assert col0 % tn == 0
    nj, j0 = c // tn, col0 // tn
    blocks = _nbytes((tm, k), BF16) + 2 * _nbytes((k, tn), BF16) + _nbytes((tm, tn), F32)
    return pl.pallas_call(
        _glu_kernel,
        out_shape=jax.ShapeDtypeStruct((m, c), F32),
        grid=(m // tm, nj),
        in_specs=[pl.BlockSpec((tm, k), lambda i, j: (i, 0)),
                  pl.BlockSpec((k, tn), lambda i, j: (0, j0 + j)),
                  pl.BlockSpec((k, tn), lambda i, j: (0, j0 + nj + j))],
        out_specs=pl.BlockSpec((tm, tn), lambda i, j: (i, j)),
        compiler_params=_params(2, 2 * blocks + 8 * _nbytes((tm, tn), F32)),
        name="glu_proj",
    )(a, w, w)


def _sb_tile(q, kblk, vblk, tri, carry, scale, mask):
    z = _dot_nt(q, kblk) * scale
    l = -(jnp.maximum(z, 0.0) + jnp.log(1.0 + jnp.exp(-jnp.abs(z))))
    if mask is not None:
        l = jnp.where(mask, l, 0.0)
    l_hi = l.astype(BF16)
    l_lo = (l - l_hi.astype(F32)).astype(BF16)
    cum = _dot(l_hi, tri) + _dot(l_lo, tri)
    a = jnp.exp(z + cum + carry)
    if mask is not None:
        a = jnp.where(mask, a, 0.0)
    pv = _dot(a.astype(BF16), vblk)
    return pv, carry + cum[:, 0:1]


def _sb_attn_kernel(q_ref, k_ref, v_ref, o_ref, acc_ref, carry_ref, *, tq, scale):
    i = pl.program_id(1)
    q = q_ref[...]
    rows = lax.broadcasted_iota(jnp.int32, (tq, tq), 0)
    cols = lax.broadcasted_iota(jnp.int32, (tq, tq), 1)
    tri = (rows >= cols).astype(BF16)
    causal = cols < rows

    def tile(block, carry, mask):
        off = pl.multiple_of(block * tq, tq)
        return _sb_tile(q, k_ref[pl.ds(off, tq), :], v_ref[pl.ds(off, tq), :],
                        tri, carry, scale, mask)

    acc, carry = tile(i, jnp.zeros((tq, 1), F32), causal)
    carry = carry + jnp.where(i > 0, 0.0, -jnp.inf)
    pv, carry = tile(jnp.maximum(i - 1, 0), carry, None)
    acc_ref[...] = acc + pv
    carry_ref[...] = carry

    def more(state):
        block, live = state
        return jnp.logical_and(block >= 0, live > SB_SKIP_LOG_WEIGHT)

    def body(state):
        block, _ = state
        pv, carry = tile(block, carry_ref[...], None)
        acc_ref[...] += pv
        carry_ref[...] = carry
        return block - 1, jnp.max(carry)

    lax.while_loop(more, body, (i - 2, jnp.max(carry)))
    o_ref[...] = acc_ref[...].astype(o_ref.dtype)


def _sb_attention(qkv, heads, tq=256):
    s = qkv.shape[0]
    dh = SB_HEAD_DIM
    tq = _tile(s, tq)
    kv_bytes = 2 * _nbytes((s, dh), BF16)
    blocks = 2 * _nbytes((tq, dh), BF16) + kv_bytes
    return pl.pallas_call(
        functools.partial(_sb_attn_kernel, tq=tq, scale=dh ** -0.5),
        out_shape=jax.ShapeDtypeStruct((s, heads * dh), BF16),
        grid=(heads, s // tq),
        in_specs=[pl.BlockSpec((tq, dh), lambda h, i: (i, h)),
                  pl.BlockSpec((s, dh), lambda h, i: (0, heads + h)),
                  pl.BlockSpec((s, dh), lambda h, i: (0, 2 * heads + h))],
        out_specs=pl.BlockSpec((tq, dh), lambda h, i: (i, h)),
        scratch_shapes=[pltpu.VMEM((tq, dh), F32), pltpu.VMEM((tq, 1), F32)],
        compiler_params=_params(2, 2 * blocks + 16 * _nbytes((tq, tq), F32)),
        name="sb_attention",
    )(qkv, qkv, qkv)


def _conv_module_kernel(halo_ref, c_ref, dw_ref, dwb_ref, g_ref, b_ref, o_ref,
                        ext_ref, y_ref, *, tm, halo, lanes):
    i = pl.program_id(0)
    kw = dw_ref.shape[0]
    ch = c_ref.shape[1]
    ext_ref[0:halo, :] = jnp.where(i > 0, halo_ref[...], 0.0)
    ext_ref[halo:halo + tm, :] = c_ref[...]
    for c0 in range(0, ch, lanes):
        acc = None
        for t in range(kw):
            off = halo - (kw - 1) + t
            term = dw_ref[t:t + 1, c0:c0 + lanes] * ext_ref[off:off + tm, c0:c0 + lanes]
            acc = term if acc is None else acc + term
        y_ref[:, c0:c0 + lanes] = acc + dwb_ref[:, c0:c0 + lanes]
    y = y_ref[...]
    mu = jnp.mean(y, axis=-1, keepdims=True)
    yc = y - mu
    var = jnp.mean(yc * yc, axis=-1, keepdims=True)
    yn = yc * lax.rsqrt(var + NORM_EPS) * g_ref[...] + b_ref[...]
    o_ref[...] = (yn * jax.nn.sigmoid(yn)).astype(o_ref.dtype)


def _conv_module(c, dw, dwb, ln_g, ln_b, tm=128):
    s, ch = c.shape
    kw = dw.shape[0]
    halo = -(-(kw - 1) // F32_SUBLANES) * F32_SUBLANES
    tm = _tile(s, tm)
    assert tm % halo == 0 or s == tm
    lanes = _tile(ch, 256)
    hb = tm // halo
    vec = pl.BlockSpec((1, ch), lambda i: (0, 0))
    blocks = (_nbytes((halo + tm, ch), F32) + _nbytes((kw, ch), F32) + _nbytes((tm, ch), BF16))
    scratch = _nbytes((halo + 2 * tm, ch), F32)
    return pl.pallas_call(
        functools.partial(_conv_module_kernel, tm=tm, halo=halo, lanes=lanes),
        out_shape=jax.ShapeDtypeStruct((s, ch), BF16),
        grid=(s // tm,),
        in_specs=[pl.BlockSpec((halo, ch), lambda i: (jnp.maximum(i * hb - 1, 0), 0)),
                  pl.BlockSpec((tm, ch), lambda i: (i, 0)),
                  pl.BlockSpec((kw, ch), lambda i: (0, 0)),
                  vec, vec, vec],
        out_specs=pl.BlockSpec((tm, ch), lambda i: (i, 0)),
        scratch_shapes=[pltpu.VMEM((halo + tm, ch), F32), pltpu.VMEM((tm, ch), F32)],
        compiler_params=_params(1, 2 * blocks + scratch + 6 * _nbytes((tm, ch), F32)),
        name="conv_module",
    )(c, c, dw, dwb.reshape(1, ch), ln_g.reshape(1, ch), ln_b.reshape(1, ch))


def _mem_attn_kernel(q_ref, k_ref, v_ref, o_ref, *, heads, scale):
    dh = q_ref.shape[1] // heads
    for h in range(heads):
        sl = slice(h * dh, (h + 1) * dh)
        s = _dot_nt(q_ref[:, sl], k_ref[:, sl]) * scale
        e = jnp.exp(s - jnp.max(s, axis=-1, keepdims=True))
        p = e / jnp.sum(e, axis=-1, keepdims=True)
        o_ref[:, sl] = _dot(p.astype(BF16), v_ref[:, sl]).astype(o_ref.dtype)


def _mem_attention(qsrc, q_col_block, kh, vh, tm=512):
    s = qsrc.shape[0]
    mtok, w = kh.shape
    tm = _tile(s, tm)
    blocks = 2 * _nbytes((tm, w), BF16) + 2 * _nbytes((mtok, w), BF16)
    return pl.pallas_call(
        functools.partial(_mem_attn_kernel, heads=MEM_HEADS, scale=(w // MEM_HEADS) ** -0.5),
        out_shape=jax.ShapeDtypeStruct((s, w), BF16),
        grid=(s // tm,),
        in_specs=[pl.BlockSpec((tm, w), lambda i: (i, q_col_block)),
                  pl.BlockSpec((mtok, w), lambda i: (0, 0)),
                  pl.BlockSpec((mtok, w), lambda i: (0, 0))],
        out_specs=pl.BlockSpec((tm, w), lambda i: (i, 0)),
        compiler_params=_params(1, 2 * blocks + 8 * _nbytes((tm, mtok), F32)),
        name="mem_attention",
    )(qsrc, kh, vh)


def _merge_kernel(h_ref, a_sb_ref, a_cv_ref, a_mem_ref, g_sb_ref, g_cv_ref, g_mem_ref,
                  w_sb_ref, w_cv_ref, w_mem_ref, o_ref):
    h = h_ref[...]

    def branch(a_ref, w_ref, g_ref):
        return jax.nn.sigmoid(_dot(h, g_ref[...])) * _dot(a_ref[...], w_ref[...])

    merged = (branch(a_sb_ref, w_sb_ref, g_sb_ref) + branch(a_cv_ref, w_cv_ref, g_cv_ref)
              + branch(a_mem_ref, w_mem_ref, g_mem_ref))
    o_ref[...] = merged.astype(o_ref.dtype)


def _gated_merge(h, a_sb, a_cv, a_mem, w_gates, gate_col0, w_sb, w_cv, w_mem, tm=1024, tn=256):
    m, d = h.shape
    tm, tn = _tile(m, tm), _tile(d, tn)
    assert gate_col0 % tn == 0
    nj, j0 = d // tn, gate_col0 // tn
    row = lambda a: pl.BlockSpec((tm, a.shape[1]), lambda i, j: (i, 0))
    gate = lambda b: pl.BlockSpec((d, tn), lambda i, j: (0, j0 + b * nj + j))
    col = lambda w: pl.BlockSpec((w.shape[0], tn), lambda i, j: (0, j))
    blocks = (_nbytes((tm, d), BF16) + 3 * _nbytes((d, tn), BF16) + _nbytes((tm, tn), BF16)
              + sum(_nbytes((tm, a.shape[1]), BF16) + _nbytes((a.shape[1], tn), BF16)
                    for a in (a_sb, a_cv, a_mem)))
    return pl.pallas_call(
        _merge_kernel,
        out_shape=jax.ShapeDtypeStruct((m, d), BF16),
        grid=(m // tm, nj),
        in_specs=[row(h), row(a_sb), row(a_cv), row(a_mem), gate(0), gate(1), gate(2),
                  col(w_sb), col(w_cv), col(w_mem)],
        out_specs=pl.BlockSpec((tm, tn), lambda i, j: (i, j)),
        compiler_params=_params(2, 2 * blocks + 12 * _nbytes((tm, tn), F32)),
        name="gated_merge",
    )(h, a_sb, a_cv, a_mem, w_gates, w_gates, w_gates, w_sb, w_cv, w_mem)


def _residual_norm_kernel(y_ref, x_ref, g_ref, gn_ref, xo_ref, ho_ref):
    xn = x_ref[...] + _rms(y_ref[...], g_ref[...])
    xo_ref[...] = xn
    ho_ref[...] = _rms(xn, gn_ref[...]).astype(ho_ref.dtype)


def _residual_norm_last_kernel(y_ref, x_ref, g_ref, xo_ref):
    xo_ref[...] = x_ref[...] + _rms(y_ref[...], g_ref[...])


def _residual_norm(y, x, g, g_next, tm=256):
    m, d = x.shape
    tm = _tile(m, tm)
    blk = pl.BlockSpec((tm, d), lambda i: (i, 0))
    vec = pl.BlockSpec((1, d), lambda i: (0, 0))
    f32_blk = _nbytes((tm, d), F32)
    if g_next is None:
        return pl.pallas_call(
            _residual_norm_last_kernel,
            out_shape=jax.ShapeDtypeStruct((m, d), F32),
            grid=(m // tm,), in_specs=[blk, blk, vec], out_specs=blk,
            compiler_params=_params(1, 10 * f32_blk),
            name="residual_norm_last",
        )(y, x, g.reshape(1, d)), None
    return pl.pallas_call(
        _residual_norm_kernel,
        out_shape=(jax.ShapeDtypeStruct((m, d), F32), jax.ShapeDtypeStruct((m, d), BF16)),
        grid=(m // tm,), in_specs=[blk, blk, vec, vec], out_specs=(blk, blk),
        compiler_params=_params(1, 12 * f32_blk),
        name="residual_norm",
    )(y, x, g.reshape(1, d), g_next.reshape(1, d))


def _ffn_up_kernel(halo_ref, h_ref, wg_ref, wv_ref, dwg_ref, dwv_ref, bg_ref, bv_ref, o_ref,
                   a_ref, p_ref, *, tm, halo):
    i = pl.program_id(0)

    @pl.when(pl.program_id(1) == 0)
    def _():
        a_ref[0:halo, :] = jnp.where(i > 0, halo_ref[...], jnp.zeros_like(halo_ref))
        a_ref[halo:halo + tm, :] = h_ref[...]

    a = a_ref[...]

    def conv(w_ref, dw_ref, b_ref):
        p_ref[...] = _dot(a, w_ref[...])
        kw = dw_ref.shape[0]
        u = None
        for t in range(kw):
            off = halo - (kw - 1) + t
            term = dw_ref[t:t + 1, :] * p_ref[off:off + tm, :]
            u = term if u is None else u + term
        return u + b_ref[...]

    u_gate = conv(wg_ref, dwg_ref, bg_ref)
    u_val = conv(wv_ref, dwv_ref, bv_ref)
    o_ref[...] = (jax.nn.gelu(u_gate, approximate=True) * u_val).astype(o_ref.dtype)


def _ffn_up(h, w_up, dw, dwb, tm=1024, tn=512):
    m, d = h.shape
    f = w_up.shape[1] // 2
    kw = dw.shape[0]
    halo = -(-(kw - 1) // BF16_SUBLANES) * BF16_SUBLANES
    tm, tn = _tile(m, tm), _tile(f, tn)
    assert tm % halo == 0
    nj, hb = f // tn, tm // halo
    wspec = lambda half: pl.BlockSpec((d, tn), lambda i, j: (0, j + half * nj))
    dwspec = lambda half: pl.BlockSpec((kw, tn), lambda i, j: (0, j + half * nj))
    bspec = lambda half: pl.BlockSpec((1, tn), lambda i, j: (0, j + half * nj))
    blocks = (_nbytes((halo + tm, d), BF16) + 2 * _nbytes((d, tn), BF16) + _nbytes((tm, tn), BF16))
    scratch = _nbytes((halo + tm, d), BF16) + _nbytes((halo + tm, tn), F32)
    dwb2 = dwb.reshape(1, 2 * f)
    return pl.pallas_call(
        functools.partial(_ffn_up_kernel, tm=tm, halo=halo),
        out_shape=jax.ShapeDtypeStruct((m, f), BF16),
        grid=(m // tm, nj),
        in_specs=[pl.BlockSpec((halo, d), lambda i, j: (jnp.maximum(i * hb - 1, 0), 0)),
                  pl.BlockSpec((tm, d), lambda i, j: (i, 0)),
                  wspec(0), wspec(1), dwspec(0), dwspec(1), bspec(0), bspec(1)],
        out_specs=pl.BlockSpec((tm, tn), lambda i, j: (i, j)),
        scratch_shapes=[pltpu.VMEM((halo + tm, d), BF16), pltpu.VMEM((halo + tm, tn), F32)],
        compiler_params=_params(2, 2 * blocks + scratch + 6 * _nbytes((tm, tn), F32)),
        name="ffn_up",
    )(h, h, w_up, w_up, dw, dw, dwb2, dwb2)


def _forward_one(x, mem, p):
    depth = p["w_in"].shape[0]
    d = x.shape[1]
    sbw = p["sb_wo"].shape[1]
    ch = p["cv_wo"].shape[1]
    mw = p["mem_wo"].shape[1]
    heads = sbw // SB_HEAD_DIM
    assert (3 * sbw) % mw == 0
    glu0, mq0, gate0 = 3 * sbw, 3 * sbw + 2 * ch, 3 * sbw + 2 * ch + mw

    h = _rmsnorm(x, p["norm_mix_pre"][0])
    for l in range(depth):
        w_in = p["w_in"][l].astype(BF16)

        qkvm = _matmul(h, w_in, BF16, "proj_qkvm", [(0, glu0), (mq0, mw)])
        c = _glu_proj(h, w_in, glu0, ch)

        a_sb = _sb_attention(qkvm, heads)
        a_cv = _conv_module(c, p["cv_dw"][l], p["cv_dw_b"][l], p["cv_ln_g"][l], p["cv_ln_b"][l])

        mem_n = _rmsnorm(mem, p["mem_norm"][l])
        kh = _matmul(mem_n, p["mem_wk"][l].astype(BF16), BF16, "mem_k")
        vh = _matmul(mem_n, p["mem_wv"][l].astype(BF16), BF16, "mem_v")
        a_mem = _mem_attention(qkvm, (3 * sbw) // mw, kh, vh)

        merged = _gated_merge(h, a_sb, a_cv, a_mem, w_in, gate0, p["sb_wo"][l].astype(BF16),
                              p["cv_wo"][l].astype(BF16), p["mem_wo"][l].astype(BF16))
        y = _matmul(merged, p["w_out"][l].astype(BF16), F32, "proj_out")
        x, h2 = _residual_norm(y, x, p["norm_mix_post"][l], p["norm_ffn_pre"][l])

        f = _ffn_up(h2, p["ffn_up"][l].astype(BF16), p["ffn_dw"][l], p["ffn_dw_b"][l])
        y = _matmul(f, p["ffn_down"][l].astype(BF16), F32, "ffn_down")
        g_next = p["norm_mix_pre"][l + 1] if l + 1 < depth else None
        x, h = _residual_norm(y, x, p["norm_ffn_post"][l], g_next)
    return x


def kernel(x, mem, norm_mix_pre, w_in, sb_wo, cv_dw, cv_dw_b, cv_ln_g, cv_ln_b, cv_wo, mem_norm, mem_wk, mem_wv, mem_wo, w_out, norm_mix_post, norm_ffn_pre, ffn_up, ffn_dw, ffn_dw_b, ffn_down, norm_ffn_post):
    p = dict(norm_mix_pre=norm_mix_pre, w_in=w_in, sb_wo=sb_wo, cv_dw=cv_dw, cv_dw_b=cv_dw_b,
             cv_ln_g=cv_ln_g, cv_ln_b=cv_ln_b, cv_wo=cv_wo, mem_norm=mem_norm, mem_wk=mem_wk,
             mem_wv=mem_wv, mem_wo=mem_wo, w_out=w_out, norm_mix_post=norm_mix_post,
             norm_ffn_pre=norm_ffn_pre, ffn_up=ffn_up, ffn_dw=ffn_dw, ffn_dw_b=ffn_dw_b,
             ffn_down=ffn_down, norm_ffn_post=norm_ffn_post)
    return jnp.stack([_forward_one(x[b], mem[b], p) for b in range(x.shape[0])])
```

```python
import functools

import jax
import jax.numpy as jnp
from jax import lax
from jax.experimental import pallas as pl
from jax.experimental.pallas import tpu as pltpu

F32 = jnp.float32
BF16 = jnp.bfloat16

NORM_EPS = 1e-6
SB_HEAD_DIM = 128
MEM_HEADS = 4

V7X_VMEM_BYTES = 64 * 1024 * 1024
VMEM_LIMIT_CAP = V7X_VMEM_BYTES - 6 * 1024 * 1024
MATMUL_BLOCK_BUDGET = 44 * 1024 * 1024
CAST_BLOCK_BYTES = 8 * 1024 * 1024
BF16_SUBLANES = 16
F32_SUBLANES = 8
SB_SKIP_LOG_WEIGHT = -128.0


def _tile(n, pref):
    t = min(n, pref)
    while n % t:
        t //= 2
    return t


def _params(n_axes, vmem_bytes):
    return pltpu.CompilerParams(
        dimension_semantics=("arbitrary",) * n_axes,
        vmem_limit_bytes=int(min(max(vmem_bytes, 16 * 1024 * 1024), VMEM_LIMIT_CAP)))


def _nbytes(shape, dtype):
    n = jnp.dtype(dtype).itemsize
    for s in shape:
        n *= s
    return n


def _dot(a, b):
    return jnp.dot(a, b, preferred_element_type=F32)


def _dot_nt(a, b):
    return lax.dot_general(a, b, (((1,), (1,)), ((), ())), preferred_element_type=F32)


def _rms(x, g):
    return x * lax.rsqrt(jnp.mean(x * x, axis=-1, keepdims=True) + NORM_EPS) * g


def _cast_kernel(x_ref, o_ref):
    o_ref[...] = x_ref[...].astype(o_ref.dtype)


def _layer_bf16(w, l):
    _, k, n = w.shape
    tr = BF16_SUBLANES
    while tr * 2 <= k and k % (tr * 2) == 0 and _nbytes((tr * 2, n), F32) <= CAST_BLOCK_BYTES:
        tr *= 2
    tr = tr if k % tr == 0 else k
    return pl.pallas_call(
        _cast_kernel,
        out_shape=jax.ShapeDtypeStruct((k, n), BF16),
        grid=(k // tr,),
        in_specs=[pl.BlockSpec((None, tr, n), lambda i: (l, i, 0))],
        out_specs=pl.BlockSpec((tr, n), lambda i: (i, 0)),
        compiler_params=_params(1, 3 * _nbytes((tr, n), F32) + 2 * _nbytes((tr, n), BF16)),
        name="cast_bf16",
    )(w)


def _rmsnorm_kernel(x_ref, g_ref, o_ref):
    o_ref[...] = _rms(x_ref[...], g_ref[...]).astype(o_ref.dtype)


def _rmsnorm(x, g, tm=512):
    m, d = x.shape
    tm = _tile(m, tm)
    blocks = _nbytes((tm, d), F32) + _nbytes((tm, d), BF16)
    return pl.pallas_call(
        _rmsnorm_kernel,
        out_shape=jax.ShapeDtypeStruct((m, d), BF16),
        grid=(m // tm,),
        in_specs=[pl.BlockSpec((tm, d), lambda i: (i, 0)),
                  pl.BlockSpec((1, d), lambda i: (0, 0))],
        out_specs=pl.BlockSpec((tm, d), lambda i: (i, 0)),
        compiler_params=_params(1, 2 * blocks + 2 * _nbytes((tm, d), F32)),
        name="rmsnorm",
    )(x, g.reshape(1, d))


def _mm_kernel(a_ref, b_ref, o_ref):
    o_ref[...] = _dot(a_ref[...], b_ref[...]).astype(o_ref.dtype)


def _matmul_tiles(m, k, n, out_dtype):
    tm, tn = _tile(m, 1024), _tile(n, 1024)

    def need(tm, tn):
        blocks = _nbytes((tm, k), BF16) + _nbytes((k, tn), BF16) + _nbytes((tm, tn), out_dtype)
        return 2 * blocks + _nbytes((tm, tn), F32)

    while need(tm, tn) > MATMUL_BLOCK_BUDGET and tn > 256:
        tn //= 2
    while need(tm, tn) > MATMUL_BLOCK_BUDGET and tm > 256:
        tm //= 2
    return tm, tn, need(tm, tn)


def _matmul(a, b, out_dtype, name, col_ranges=None):
    m, k = a.shape
    col_ranges = col_ranges or [(0, b.shape[1])]
    n = sum(w for _, w in col_ranges)
    tm, tn, need = _matmul_tiles(m, k, n, out_dtype)
    while any(c % tn or w % tn for c, w in col_ranges):
        tn //= 2

    def b_block(j):
        blk, first = j, 0
        for c0, w in col_ranges:
            blk = jnp.where(j >= first, j - first + c0 // tn, blk)
            first += w // tn
        return blk

    return pl.pallas_call(
        _mm_kernel,
        out_shape=jax.ShapeDtypeStruct((m, n), out_dtype),
        grid=(m // tm, n // tn),
        in_specs=[pl.BlockSpec((tm, k), lambda i, j: (i, 0)),
                  pl.BlockSpec((k, tn), lambda i, j: (0, b_block(j)))],
        out_specs=pl.BlockSpec((tm, tn), lambda i, j: (i, j)),
        compiler_params=_params(2, need + 4 * 1024 * 1024),
        name=name,
    )(a, b)


def _glu_kernel(a_ref, wa_ref, wb_ref, o_ref):
    a = a_ref[...]
    o_ref[...] = _dot(a, wa_ref[...]) * jax.nn.sigmoid(_dot(a, wb_ref[...]))


def _glu_proj(a, w, col0, c, tm=1024, tn=512):
    m, k = a.shape
    tm, tn = _tile(m, tm), _tile(c, tn)
    assert col0 % tn == 0
    nj, j0 = c // tn, col0 // tn
    blocks = _nbytes((tm, k), BF16) + 2 * _nbytes((k, tn), BF16) + _nbytes((tm, tn), F32)
    return pl.pallas_call(
        _glu_kernel,
        out_shape=jax.ShapeDtypeStruct((m, c), F32),
        grid=(m // tm, nj),
        in_specs=[pl.BlockSpec((tm, k), lambda i, j: (i, 0)),
                  pl.BlockSpec((k, tn), lambda i, j: (0, j0 + j)),
                  pl.BlockSpec((k, tn), lambda i, j: (0, j0 + nj + j))],
        out_specs=pl.BlockSpec((tm, tn), lambda i, j: (i, j)),
        compiler_params=_params(2, 2 * blocks + 8 * _nbytes((tm, tn), F32)),
        name="glu_proj",
    )(a, w, w)


def _sb_tile(q, kblk, vblk, tri, carry, scale, mask):
    z = _dot_nt(q, kblk) * scale
    l = -(jnp.maximum(z, 0.0) + jnp.log(1.0 + jnp.exp(-jnp.abs(z))))
    if mask is not None:
        l = jnp.where(mask, l, 0.0)
    l_hi = l.astype(BF16)
    l_lo = (l - l_hi.astype(F32)).astype(BF16)
    cum = _dot(l_hi, tri) + _dot(l_lo, tri)
    a = jnp.exp(z + cum + carry)
    if mask is not None:
        a = jnp.where(mask, a, 0.0)
    pv = _dot(a.astype(BF16), vblk)
    return pv, carry + cum[:, 0:1]


def _sb_attn_kernel(q_ref, k_ref, v_ref, o_ref, acc_ref, carry_ref, *, tq, dh, group, scale):
    i = pl.program_id(1)
    rows = lax.broadcasted_iota(jnp.int32, (tq, tq), 0)
    cols = lax.broadcasted_iota(jnp.int32, (tq, tq), 1)
    tri = (rows >= cols).astype(BF16)
    causal = cols < rows
    heads = [slice(g * dh, (g + 1) * dh) for g in range(group)]
    qs = [q_ref[:, hd] for hd in heads]

    def tile(g, block, carry, mask):
        off = pl.multiple_of(block * tq, tq)
        return _sb_tile(qs[g], k_ref[pl.ds(off, tq), heads[g]], v_ref[pl.ds(off, tq), heads[g]],
                        tri, carry, scale, mask)

    live = []
    for g in range(group):
        acc, carry = tile(g, i, jnp.zeros((tq, 1), F32), causal)
        carry = carry + jnp.where(i > 0, 0.0, -jnp.inf)
        pv, carry = tile(g, jnp.maximum(i - 1, 0), carry, None)
        acc_ref[g] = acc + pv
        carry_ref[g] = carry
        live.append(jnp.max(carry))

    def more(state):
        block, live = state
        return jnp.logical_and(block >= 0, live > SB_SKIP_LOG_WEIGHT)

    for g in range(group):
        def body(state, g=g):
            block, _ = state
            pv, carry = tile(g, block, carry_ref[g], None)
            acc_ref[g] += pv
            carry_ref[g] = carry
            return block - 1, jnp.max(carry)

        lax.while_loop(more, body, (i - 2, live[g]))
        o_ref[:, heads[g]] = acc_ref[g].astype(o_ref.dtype)


def _sb_attention(qkv, heads, tq=256):
    s = qkv.shape[0]
    dh = SB_HEAD_DIM
    tq = _tile(s, tq)
    group = 2 if heads % 2 == 0 else 1
    w, ng = group * dh, heads // group
    blocks = 2 * _nbytes((tq, w), BF16) + 2 * _nbytes((s, w), BF16)
    return pl.pallas_call(
        functools.partial(_sb_attn_kernel, tq=tq, dh=dh, group=group, scale=dh ** -0.5),
        out_shape=jax.ShapeDtypeStruct((s, heads * dh), BF16),
        grid=(ng, s // tq),
        in_specs=[pl.BlockSpec((tq, w), lambda h, i: (i, h)),
                  pl.BlockSpec((s, w), lambda h, i: (0, ng + h)),
                  pl.BlockSpec((s, w), lambda h, i: (0, 2 * ng + h))],
        out_specs=pl.BlockSpec((tq, w), lambda h, i: (i, h)),
        scratch_shapes=[pltpu.VMEM((group, tq, dh), F32), pltpu.VMEM((group, tq, 1), F32)],
        compiler_params=_params(2, 2 * blocks + 24 * group * _nbytes((tq, tq), F32)),
        name="sb_attention",
    )(qkv, qkv, qkv)


def _conv_module_kernel(halo_ref, c_ref, dw_ref, dwb_ref, g_ref, b_ref, o_ref,
                        ext_ref, shift_ref, y_ref, *, tm, halo, lanes):
    i = pl.program_id(0)
    kw = dw_ref.shape[0]
    ch = c_ref.shape[1]
    sub = F32_SUBLANES
    span = halo + tm - sub
    ext_ref[0:halo, :] = jnp.where(i > 0, halo_ref[...], 0.0)
    ext_ref[halo:halo + tm, :] = c_ref[...]
    for r in range(1, sub):
        shift_ref[r - 1] = ext_ref[r:r + span, :]
    for c0 in range(0, ch, lanes):
        acc = None
        for t in range(kw):
            off = halo - (kw - 1) + t
            r, base = off % sub, off - off % sub
            if r == 0:
                win = ext_ref[base:base + tm, c0:c0 + lanes]
            else:
                win = shift_ref[r - 1, base:base + tm, c0:c0 + lanes]
            term = dw_ref[t:t + 1, c0:c0 + lanes] * win
            acc = term if acc is None else acc + term
        y_ref[:, c0:c0 + lanes] = acc + dwb_ref[:, c0:c0 + lanes]
    y = y_ref[...]
    mu = jnp.mean(y, axis=-1, keepdims=True)
    yc = y - mu
    var = jnp.mean(yc * yc, axis=-1, keepdims=True)
    yn = yc * lax.rsqrt(var + NORM_EPS) * g_ref[...] + b_ref[...]
    o_ref[...] = (yn * jax.nn.sigmoid(yn)).astype(o_ref.dtype)


def _conv_module(c, dw, dwb, ln_g, ln_b, tm=256):
    s, ch = c.shape
    kw = dw.shape[0]
    halo = -(-(kw - 1) // F32_SUBLANES) * F32_SUBLANES
    tm = _tile(s, tm)
    assert tm % halo == 0 or s == tm
    lanes = _tile(ch, 128)
    hb = tm // halo
    vec = pl.BlockSpec((1, ch), lambda i: (0, 0))
    blocks = (_nbytes((halo + tm, ch), F32) + _nbytes((kw, ch), F32) + _nbytes((tm, ch), BF16))
    shift_shape = (F32_SUBLANES - 1, halo + tm - F32_SUBLANES, ch)
    scratch = _nbytes((halo + 2 * tm, ch), F32) + _nbytes(shift_shape, F32)
    return pl.pallas_call(
        functools.partial(_conv_module_kernel, tm=tm, halo=halo, lanes=lanes),
        out_shape=jax.ShapeDtypeStruct((s, ch), BF16),
        grid=(s // tm,),
        in_specs=[pl.BlockSpec((halo, ch), lambda i: (jnp.maximum(i * hb - 1, 0), 0)),
                  pl.BlockSpec((tm, ch), lambda i: (i, 0)),
                  pl.BlockSpec((kw, ch), lambda i: (0, 0)),
                  vec, vec, vec],
        out_specs=pl.BlockSpec((tm, ch), lambda i: (i, 0)),
        scratch_shapes=[pltpu.VMEM((halo + tm, ch), F32), pltpu.VMEM(shift_shape, F32),
                        pltpu.VMEM((tm, ch), F32)],
        compiler_params=_params(1, 2 * blocks + scratch + 6 * _nbytes((tm, ch), F32)),
        name="conv_module",
    )(c, c, dw, dwb.reshape(1, ch), ln_g.reshape(1, ch), ln_b.reshape(1, ch))


def _mem_attn_kernel(q_ref, k_ref, v_ref, o_ref, *, heads, scale):
    dh = q_ref.shape[1] // heads
    for h in range(heads):
        sl = slice(h * dh, (h + 1) * dh)
        s = _dot_nt(q_ref[:, sl], k_ref[:, sl]) * scale
        e = jnp.exp(s - jnp.max(s, axis=-1, keepdims=True))
        p = e / jnp.sum(e, axis=-1, keepdims=True)
        o_ref[:, sl] = _dot(p.astype(BF16), v_ref[:, sl]).astype(o_ref.dtype)


def _mem_attention(qsrc, q_col_block, kh, vh, tm=512):
    s = qsrc.shape[0]
    mtok, w = kh.shape
    tm = _tile(s, tm)
    blocks = 2 * _nbytes((tm, w), BF16) + 2 * _nbytes((mtok, w), BF16)
    return pl.pallas_call(
        functools.partial(_mem_attn_kernel, heads=MEM_HEADS, scale=(w // MEM_HEADS) ** -0.5),
        out_shape=jax.ShapeDtypeStruct((s, w), BF16),
        grid=(s // tm,),
        in_specs=[pl.BlockSpec((tm, w), lambda i: (i, q_col_block)),
                  pl.BlockSpec((mtok, w), lambda i: (0, 0)),
                  pl.BlockSpec((mtok, w), lambda i: (0, 0))],
        out_specs=pl.BlockSpec((tm, w), lambda i: (i, 0)),
        compiler_params=_params(1, 2 * blocks + 8 * _nbytes((tm, mtok), F32)),
        name="mem_attention",
    )(qsrc, kh, vh)


def _merge_kernel(h_ref, a_sb_ref, a_cv_ref, a_mem_ref, g_sb_ref, g_cv_ref, g_mem_ref,
                  w_sb_ref, w_cv_ref, w_mem_ref, o_ref):
    h = h_ref[...]

    def branch(a_ref, w_ref, g_ref):
        return jax.nn.sigmoid(_dot(h, g_ref[...])) * _dot(a_ref[...], w_ref[...])

    merged = (branch(a_sb_ref, w_sb_ref, g_sb_ref) + branch(a_cv_ref, w_cv_ref, g_cv_ref)
              + branch(a_mem_ref, w_mem_ref, g_mem_ref))
    o_ref[...] = merged.astype(o_ref.dtype)


def _gated_merge(h, a_sb, a_cv, a_mem, w_gates, gate_col0, w_sb, w_cv, w_mem, tm=1024, tn=256):
    m, d = h.shape
    tm, tn = _tile(m, tm), _tile(d, tn)
    assert gate_col0 % tn == 0
    nj, j0 = d // tn, gate_col0 // tn
    row = lambda a: pl.BlockSpec((tm, a.shape[1]), lambda i, j: (i, 0))
    gate = lambda b: pl.BlockSpec((d, tn), lambda i, j: (0, j0 + b * nj + j))
    col = lambda w: pl.BlockSpec((w.shape[0], tn), lambda i, j: (0, j))
    blocks = (_nbytes((tm, d), BF16) + 3 * _nbytes((d, tn), BF16) + _nbytes((tm, tn), BF16)
              + sum(_nbytes((tm, a.shape[1]), BF16) + _nbytes((a.shape[1], tn), BF16)
                    for a in (a_sb, a_cv, a_mem)))
    return pl.pallas_call(
        _merge_kernel,
        out_shape=jax.ShapeDtypeStruct((m, d), BF16),
        grid=(m // tm, nj),
        in_specs=[row(h), row(a_sb), row(a_cv), row(a_mem), gate(0), gate(1), gate(2),
                  col(w_sb), col(w_cv), col(w_mem)],
        out_specs=pl.BlockSpec((tm, tn), lambda i, j: (i, j)),
        compiler_params=_params(2, 2 * blocks + 12 * _nbytes((tm, tn), F32)),
        name="gated_merge",
    )(h, a_sb, a_cv, a_mem, w_gates, w_gates, w_gates, w_sb, w_cv, w_mem)


def _residual_norm_kernel(y_ref, x_ref, g_ref, gn_ref, xo_ref, ho_ref):
    xn = x_ref[...] + _rms(y_ref[...], g_ref[...])
    xo_ref[...] = xn
    ho_ref[...] = _rms(xn, gn_ref[...]).astype(ho_ref.dtype)


def _residual_norm_last_kernel(y_ref, x_ref, g_ref, xo_ref):
    xo_ref[...] = x_ref[...] + _rms(y_ref[...], g_ref[...])


def _residual_norm(y, x, g, g_next, tm=256):
    m, d = x.shape
    tm = _tile(m, tm)
    blk = pl.BlockSpec((tm, d), lambda i: (i, 0))
    vec = pl.BlockSpec((1, d), lambda i: (0, 0))
    f32_blk = _nbytes((tm, d), F32)
    if g_next is None:
        return pl.pallas_call(
            _residual_norm_last_kernel,
            out_shape=jax.ShapeDtypeStruct((m, d), F32),
            grid=(m // tm,), in_specs=[blk, blk, vec], out_specs=blk,
            compiler_params=_params(1, 10 * f32_blk),
            name="residual_norm_last",
        )(y, x, g.reshape(1, d)), None
    return pl.pallas_call(
        _residual_norm_kernel,
        out_shape=(jax.ShapeDtypeStruct((m, d), F32), jax.ShapeDtypeStruct((m, d), BF16)),
        grid=(m // tm,), in_specs=[blk, blk, vec, vec], out_specs=(blk, blk),
        compiler_params=_params(1, 12 * f32_blk),
        name="residual_norm",
    )(y, x, g.reshape(1, d), g_next.reshape(1, d))


def _ffn_up_kernel(halo_ref, h_ref, wg_ref, wv_ref, dwg_ref, dwv_ref, bg_ref, bv_ref, o_ref,
                   a_ref, pg0_ref, pv0_ref, pg1_ref, pv1_ref, *, tm, halo, nj, n_tiles):
    n = pl.program_id(0)
    kw = dwg_ref.shape[0]
    rows = 2 * BF16_SUBLANES

    @pl.when(n == 0)
    def _():
        pg1_ref[...] = jnp.zeros_like(pg1_ref)
        pv1_ref[...] = jnp.zeros_like(pv1_ref)

    @pl.when(jnp.logical_and(n % nj == 0, n < n_tiles))
    def _():
        a_ref[0:halo, :] = jnp.where(n > 0, halo_ref[...], jnp.zeros_like(halo_ref))
        a_ref[halo:halo + tm, :] = h_ref[...]

    def conv(p_ref, dw_ref, b_ref, r0):
        u = None
        for t in range(kw):
            off = halo - (kw - 1) + t + r0
            term = dw_ref[t:t + 1, :] * p_ref[off:off + rows, :]
            u = term if u is None else u + term
        return u + b_ref[...]

    def step(pg_ref, pv_ref, pg_done_ref, pv_done_ref):
        for r0 in range(0, tm, rows):
            u_gate = conv(pg_done_ref, dwg_ref, bg_ref, r0)
            u_val = conv(pv_done_ref, dwv_ref, bv_ref, r0)
            o_ref[r0:r0 + rows, :] = (jax.nn.gelu(u_gate, approximate=True) * u_val).astype(o_ref.dtype)
        a = a_ref[...]
        pg_ref[...] = _dot(a, wg_ref[...])
        pv_ref[...] = _dot(a, wv_ref[...])

    @pl.when(n % 2 == 0)
    def _():
        step(pg0_ref, pv0_ref, pg1_ref, pv1_ref)

    @pl.when(n % 2 == 1)
    def _():
        step(pg1_ref, pv1_ref, pg0_ref, pv0_ref)


def _ffn_up(h, w_up, dw, dwb, tm=1024, tn=512):
    m, d = h.shape
    f = w_up.shape[1] // 2
    kw = dw.shape[0]
    halo = -(-(kw - 1) // BF16_SUBLANES) * BF16_SUBLANES
    tm, tn = _tile(m, tm), _tile(f, tn)
    assert tm % halo == 0
    nj, hb = f // tn, tm // halo
    n_tiles = (m // tm) * nj
    mm = lambda n: jnp.minimum(n, n_tiles - 1)
    ep = lambda n: jnp.maximum(n - 1, 0)
    wspec = lambda half: pl.BlockSpec((d, tn), lambda n: (0, mm(n) % nj + half * nj))
    dwspec = lambda half: pl.BlockSpec((kw, tn), lambda n: (0, ep(n) % nj + half * nj))
    bspec = lambda half: pl.BlockSpec((1, tn), lambda n: (0, ep(n) % nj + half * nj))
    blocks = 2 * _nbytes((d, tn), BF16) + _nbytes((tm, tn), BF16)
    p_scratch = pltpu.VMEM((halo + tm, tn), F32)
    scratch = 2 * _nbytes((halo + tm, d), BF16) + 4 * _nbytes((halo + tm, tn), F32)
    dwb2 = dwb.reshape(1, 2 * f)
    return pl.pallas_call(
        functools.partial(_ffn_up_kernel, tm=tm, halo=halo, nj=nj, n_tiles=n_tiles),
        out_shape=jax.ShapeDtypeStruct((m, f), BF16),
        grid=(n_tiles + 1,),
        in_specs=[pl.BlockSpec((halo, d), lambda n: (jnp.maximum(mm(n) // nj * hb - 1, 0), 0)),
                  pl.BlockSpec((tm, d), lambda n: (mm(n) // nj, 0), pipeline_mode=pl.Buffered(1)),
                  wspec(0), wspec(1), dwspec(0), dwspec(1), bspec(0), bspec(1)],
        out_specs=pl.BlockSpec((tm, tn), lambda n: (ep(n) // nj, ep(n) % nj)),
        scratch_shapes=[pltpu.VMEM((halo + tm, d), BF16)] + [p_scratch] * 4,
        compiler_params=_params(1, 2 * blocks + scratch + 8 * _nbytes((tm, tn), F32)),
        name="ffn_up",
    )(h, h, w_up, w_up, dw, dw, dwb2, dwb2)


def _forward_one(x, mem, p):
    depth = p["w_in"].shape[0]
    d = x.shape[1]
    sbw = p["sb_wo"].shape[1]
    ch = p["cv_wo"].shape[1]
    mw = p["mem_wo"].shape[1]
    heads = sbw // SB_HEAD_DIM
    assert (3 * sbw) % mw == 0
    glu0, mq0, gate0 = 3 * sbw, 3 * sbw + 2 * ch, 3 * sbw + 2 * ch + mw

    h = _rmsnorm(x, p["norm_mix_pre"][0])
    for l in range(depth):
        w_in = _layer_bf16(p["w_in"], l)

        qkvm = _matmul(h, w_in, BF16, "proj_qkvm", [(0, glu0), (mq0, mw)])
        c = _glu_proj(h, w_in, glu0, ch)

        a_sb = _sb_attention(qkvm, heads)
        a_cv = _conv_module(c, p["cv_dw"][l], p["cv_dw_b"][l], p["cv_ln_g"][l], p["cv_ln_b"][l])

        mem_n = _rmsnorm(mem, p["mem_norm"][l])
        kh = _matmul(mem_n, _layer_bf16(p["mem_wk"], l), BF16, "mem_k")
        vh = _matmul(mem_n, _layer_bf16(p["mem_wv"], l), BF16, "mem_v")
        a_mem = _mem_attention(qkvm, (3 * sbw) // mw, kh, vh)

        merged = _gated_merge(h, a_sb, a_cv, a_mem, w_in, gate0, _layer_bf16(p["sb_wo"], l),
                              _layer_bf16(p["cv_wo"], l), _layer_bf16(p["mem_wo"], l))
        y = _matmul(merged, _layer_bf16(p["w_out"], l), F32, "proj_out")
        x, h2 = _residual_norm(y, x, p["norm_mix_post"][l], p["norm_ffn_pre"][l])

        f = _ffn_up(h2, _layer_bf16(p["ffn_up"], l), p["ffn_dw"][l], p["ffn_dw_b"][l])
        y = _matmul(f, _layer_bf16(p["ffn_down"], l), F32, "ffn_down")
        g_next = p["norm_mix_pre"][l + 1] if l + 1 < depth else None
        x, h = _residual_norm(y, x, p["norm_ffn_post"][l], g_next)
    return x


def kernel(x, mem, norm_mix_pre, w_in, sb_wo, cv_dw, cv_dw_b, cv_ln_g, cv_ln_b, cv_wo, mem_norm, mem_wk, mem_wv, mem_wo, w_out, norm_mix_post, norm_ffn_pre, ffn_up, ffn_dw, ffn_dw_b, ffn_down, norm_ffn_post):
    p = dict(norm_mix_pre=norm_mix_pre, w_in=w_in, sb_wo=sb_wo, cv_dw=cv_dw, cv_dw_b=cv_dw_b,
             cv_ln_g=cv_ln_g, cv_ln_b=cv_ln_b, cv_wo=cv_wo, mem_norm=mem_norm, mem_wk=mem_wk,
             mem_wv=mem_wv, mem_wo=mem_wo, w_out=w_out, norm_mix_post=norm_mix_post,
             norm_ffn_pre=norm_ffn_pre, ffn_up=ffn_up, ffn_dw=ffn_dw, ffn_dw_b=ffn_dw_b,
             ffn_down=ffn_down, norm_ffn_post=norm_ffn_post)
    return jnp.stack([_forward_one(x[b], mem[b], p) for b in range(x.shape[0])])
```

```python
import functools

import jax
import jax.numpy as jnp
from jax import lax
from jax.experimental import pallas as pl
from jax.experimental.pallas import tpu as pltpu

F32 = jnp.float32
BF16 = jnp.bfloat16

NORM_EPS = 1e-6
SB_HEAD_DIM = 128
MEM_HEADS = 4

V7X_VMEM_BYTES = 64 * 1024 * 1024
VMEM_LIMIT_CAP = V7X_VMEM_BYTES - 6 * 1024 * 1024
MATMUL_BLOCK_BUDGET = 44 * 1024 * 1024
CAST_BLOCK_BYTES = 8 * 1024 * 1024
BF16_SUBLANES = 16
F32_SUBLANES = 8
SB_SKIP_LOG_WEIGHT = -128.0


def _tile(n, pref):
    t = min(n, pref)
    while n % t:
        t //= 2
    return t


def _params(n_axes, vmem_bytes):
    return pltpu.CompilerParams(
        dimension_semantics=("arbitrary",) * n_axes,
        vmem_limit_bytes=int(min(max(vmem_bytes, 16 * 1024 * 1024), VMEM_LIMIT_CAP)))


def _nbytes(shape, dtype):
    n = jnp.dtype(dtype).itemsize
    for s in shape:
        n *= s
    return n


def _dot(a, b):
    return jnp.dot(a, b, preferred_element_type=F32)


def _dot_nt(a, b):
    return lax.dot_general(a, b, (((1,), (1,)), ((), ())), preferred_element_type=F32)


def _rms(x, g):
    return x * lax.rsqrt(jnp.mean(x * x, axis=-1, keepdims=True) + NORM_EPS) * g


def _cast_kernel(x_ref, o_ref):
    o_ref[...] = x_ref[...].astype(o_ref.dtype)


def _layer_bf16(w, l):
    _, k, n = w.shape
    tr = BF16_SUBLANES
    while tr * 2 <= k and k % (tr * 2) == 0 and _nbytes((tr * 2, n), F32) <= CAST_BLOCK_BYTES:
        tr *= 2
    tr = tr if k % tr == 0 else k
    return pl.pallas_call(
        _cast_kernel,
        out_shape=jax.ShapeDtypeStruct((k, n), BF16),
        grid=(k // tr,),
        in_specs=[pl.BlockSpec((None, tr, n), lambda i: (l, i, 0))],
        out_specs=pl.BlockSpec((tr, n), lambda i: (i, 0)),
        compiler_params=_params(1, 3 * _nbytes((tr, n), F32) + 2 * _nbytes((tr, n), BF16)),
        name="cast_bf16",
    )(w)


def _rmsnorm_kernel(x_ref, g_ref, o_ref):
    o_ref[...] = _rms(x_ref[...], g_ref[...]).astype(o_ref.dtype)


def _rmsnorm(x, g, tm=512):
    m, d = x.shape
    tm = _tile(m, tm)
    blocks = _nbytes((tm, d), F32) + _nbytes((tm, d), BF16)
    return pl.pallas_call(
        _rmsnorm_kernel,
        out_shape=jax.ShapeDtypeStruct((m, d), BF16),
        grid=(m // tm,),
        in_specs=[pl.BlockSpec((tm, d), lambda i: (i, 0)),
                  pl.BlockSpec((1, d), lambda i: (0, 0))],
        out_specs=pl.BlockSpec((tm, d), lambda i: (i, 0)),
        compiler_params=_params(1, 2 * blocks + 2 * _nbytes((tm, d), F32)),
        name="rmsnorm",
    )(x, g.reshape(1, d))


def _mm_kernel(a_ref, b_ref, o_ref):
    o_ref[...] = _dot(a_ref[...], b_ref[...]).astype(o_ref.dtype)


def _matmul_tiles(m, k, n, out_dtype):
    tm, tn = _tile(m, 1024), _tile(n, 1024)

    def need(tm, tn):
        blocks = _nbytes((tm, k), BF16) + _nbytes((k, tn), BF16) + _nbytes((tm, tn), out_dtype)
        return 2 * blocks + _nbytes((tm, tn), F32)

    while need(tm, tn) > MATMUL_BLOCK_BUDGET and tn > 256:
        tn //= 2
    while need(tm, tn) > MATMUL_BLOCK_BUDGET and tm > 256:
        tm //= 2
    return tm, tn, need(tm, tn)


def _matmul(a, b, out_dtype, name, col_ranges=None):
    m, k = a.shape
    col_ranges = col_ranges or [(0, b.shape[1])]
    n = sum(w for _, w in col_ranges)
    tm, tn, need = _matmul_tiles(m, k, n, out_dtype)
    while any(c % tn or w % tn for c, w in col_ranges):
        tn //= 2

    def b_block(j):
        blk, first = j, 0
        for c0, w in col_ranges:
            blk = jnp.where(j >= first, j - first + c0 // tn, blk)
            first += w // tn
        return blk

    return pl.pallas_call(
        _mm_kernel,
        out_shape=jax.ShapeDtypeStruct((m, n), out_dtype),
        grid=(m // tm, n // tn),
        in_specs=[pl.BlockSpec((tm, k), lambda i, j: (i, 0)),
                  pl.BlockSpec((k, tn), lambda i, j: (0, b_block(j)))],
        out_specs=pl.BlockSpec((tm, tn), lambda i, j: (i, j)),
        compiler_params=_params(2, need + 4 * 1024 * 1024),
        name=name,
    )(a, b)


def _glu_kernel(a_ref, wa_ref, wb_ref, o_ref):
    a = a_ref[...]
    o_ref[...] = _dot(a, wa_ref[...]) * jax.nn.sigmoid(_dot(a, wb_ref[...]))


def _glu_proj(a, w, col0, c, tm=1024, tn=512):
    m, k = a.shape
    tm, tn = _tile(m, tm), _tile(c, tn)
    assert col0 % tn == 0
    nj, j0 = c // tn, col0 // tn
    blocks = _nbytes((tm, k), BF16) + 2 * _nbytes((k, tn), BF16) + _nbytes((tm, tn), F32)
    return pl.pallas_call(
        _glu_kernel,
        out_shape=jax.ShapeDtypeStruct((m, c), F32),
        grid=(m // tm, nj),
        in_specs=[pl.BlockSpec((tm, k), lambda i, j: (i, 0)),
                  pl.BlockSpec((k, tn), lambda i, j: (0, j0 + j)),
                  pl.BlockSpec((k, tn), lambda i, j: (0, j0 + nj + j))],
        out_specs=pl.BlockSpec((tm, tn), lambda i, j: (i, j)),
        compiler_params=_params(2, 2 * blocks + 8 * _nbytes((tm, tn), F32)),
        name="glu_proj",
    )(a, w, w)


def _sb_tile(q, kblk, vblk, tri, carry, scale, mask):
    z = _dot_nt(q, kblk) * scale
    l = -(jnp.maximum(z, 0.0) + jnp.log(1.0 + jnp.exp(-jnp.abs(z))))
    if mask is not None:
        l = jnp.where(mask, l, 0.0)
    l_hi = l.astype(BF16)
    l_lo = (l - l_hi.astype(F32)).astype(BF16)
    cum = _dot(l_hi, tri) + _dot(l_lo, tri)
    a = jnp.exp(z + cum + carry)
    if mask is not None:
        a = jnp.where(mask, a, 0.0)
    pv = _dot(a.astype(BF16), vblk)
    return pv, carry + cum[:, 0:1]


def _sb_attn_kernel(q_ref, k_ref, v_ref, o_ref, acc_ref, carry_ref, *, tq, dh, group, scale):
    i = pl.program_id(1)
    rows = lax.broadcasted_iota(jnp.int32, (tq, tq), 0)
    cols = lax.broadcasted_iota(jnp.int32, (tq, tq), 1)
    tri = (rows >= cols).astype(BF16)
    causal = cols < rows
    heads = [slice(g * dh, (g + 1) * dh) for g in range(group)]
    qs = [q_ref[:, hd] for hd in heads]

    def tile(g, block, carry, mask):
        off = pl.multiple_of(block * tq, tq)
        return _sb_tile(qs[g], k_ref[pl.ds(off, tq), heads[g]], v_ref[pl.ds(off, tq), heads[g]],
                        tri, carry, scale, mask)

    live = []
    for g in range(group):
        acc, carry = tile(g, i, jnp.zeros((tq, 1), F32), causal)
        carry = carry + jnp.where(i > 0, 0.0, -jnp.inf)
        pv, carry = tile(g, jnp.maximum(i - 1, 0), carry, None)
        acc_ref[g] = acc + pv
        carry_ref[g] = carry
        live.append(jnp.max(carry))

    def more(state):
        block, live = state
        return jnp.logical_and(block >= 0, live > SB_SKIP_LOG_WEIGHT)

    for g in range(group):
        def body(state, g=g):
            block, _ = state
            pv, carry = tile(g, block, carry_ref[g], None)
            acc_ref[g] += pv
            carry_ref[g] = carry
            return block - 1, jnp.max(carry)

        lax.while_loop(more, body, (i - 2, live[g]))
        o_ref[:, heads[g]] = acc_ref[g].astype(o_ref.dtype)


def _sb_attention(qkv, heads, tq=256):
    s = qkv.shape[0]
    dh = SB_HEAD_DIM
    tq = _tile(s, tq)
    group = 2 if heads % 2 == 0 else 1
    w, ng = group * dh, heads // group
    blocks = 2 * _nbytes((tq, w), BF16) + 2 * _nbytes((s, w), BF16)
    return pl.pallas_call(
        functools.partial(_sb_attn_kernel, tq=tq, dh=dh, group=group, scale=dh ** -0.5),
        out_shape=jax.ShapeDtypeStruct((s, heads * dh), BF16),
        grid=(ng, s // tq),
        in_specs=[pl.BlockSpec((tq, w), lambda h, i: (i, h)),
                  pl.BlockSpec((s, w), lambda h, i: (0, ng + h)),
                  pl.BlockSpec((s, w), lambda h, i: (0, 2 * ng + h))],
        out_specs=pl.BlockSpec((tq, w), lambda h, i: (i, h)),
        scratch_shapes=[pltpu.VMEM((group, tq, dh), F32), pltpu.VMEM((group, tq, 1), F32)],
        compiler_params=_params(2, 2 * blocks + 24 * group * _nbytes((tq, tq), F32)),
        name="sb_attention",
    )(qkv, qkv, qkv)


def _conv_module_kernel(halo_ref, c_ref, dw_ref, dwb_ref, g_ref, b_ref, o_ref,
                        ext_ref, shift_ref, y_ref, *, tm, halo, lanes):
    i = pl.program_id(0)
    kw = dw_ref.shape[0]
    ch = c_ref.shape[1]
    sub = F32_SUBLANES
    span = halo + tm - sub
    ext_ref[0:halo, :] = jnp.where(i > 0, halo_ref[...], 0.0)
    ext_ref[halo:halo + tm, :] = c_ref[...]
    for r in range(1, sub):
        shift_ref[r - 1] = ext_ref[r:r + span, :]
    for c0 in range(0, ch, lanes):
        acc = None
        for t in range(kw):
            off = halo - (kw - 1) + t
            r, base = off % sub, off - off % sub
            if r == 0:
                win = ext_ref[base:base + tm, c0:c0 + lanes]
            else:
                win = shift_ref[r - 1, base:base + tm, c0:c0 + lanes]
            term = dw_ref[t:t + 1, c0:c0 + lanes] * win
            acc = term if acc is None else acc + term
        y_ref[:, c0:c0 + lanes] = acc + dwb_ref[:, c0:c0 + lanes]
    y = y_ref[...]
    mu = jnp.mean(y, axis=-1, keepdims=True)
    yc = y - mu
    var = jnp.mean(yc * yc, axis=-1, keepdims=True)
    yn = yc * lax.rsqrt(var + NORM_EPS) * g_ref[...] + b_ref[...]
    o_ref[...] = (yn * jax.nn.sigmoid(yn)).astype(o_ref.dtype)


def _conv_module(c, dw, dwb, ln_g, ln_b, tm=256):
    s, ch = c.shape
    kw = dw.shape[0]
    halo = -(-(kw - 1) // F32_SUBLANES) * F32_SUBLANES
    tm = _tile(s, tm)
    assert tm % halo == 0 or s == tm
    lanes = _tile(ch, 128)
    hb = tm // halo
    vec = pl.BlockSpec((1, ch), lambda i: (0, 0))
    blocks = (_nbytes((halo + tm, ch), F32) + _nbytes((kw, ch), F32) + _nbytes((tm, ch), BF16))
    shift_shape = (F32_SUBLANES - 1, halo + tm - F32_SUBLANES, ch)
    scratch = _nbytes((halo + 2 * tm, ch), F32) + _nbytes(shift_shape, F32)
    return pl.pallas_call(
        functools.partial(_conv_module_kernel, tm=tm, halo=halo, lanes=lanes),
        out_shape=jax.ShapeDtypeStruct((s, ch), BF16),
        grid=(s // tm,),
        in_specs=[pl.BlockSpec((halo, ch), lambda i: (jnp.maximum(i * hb - 1, 0), 0)),
                  pl.BlockSpec((tm, ch), lambda i: (i, 0)),
                  pl.BlockSpec((kw, ch), lambda i: (0, 0)),
                  vec, vec, vec],
        out_specs=pl.BlockSpec((tm, ch), lambda i: (i, 0)),
        scratch_shapes=[pltpu.VMEM((halo + tm, ch), F32), pltpu.VMEM(shift_shape, F32),
                        pltpu.VMEM((tm, ch), F32)],
        compiler_params=_params(1, 2 * blocks + scratch + 6 * _nbytes((tm, ch), F32)),
        name="conv_module",
    )(c, c, dw, dwb.reshape(1, ch), ln_g.reshape(1, ch), ln_b.reshape(1, ch))


def _mem_attn_kernel(q_ref, k_ref, v_ref, o_ref, *, heads, scale):
    dh = q_ref.shape[1] // heads
    for h in range(heads):
        sl = slice(h * dh, (h + 1) * dh)
        s = _dot_nt(q_ref[:, sl], k_ref[:, sl]) * scale
        e = jnp.exp(s - jnp.max(s, axis=-1, keepdims=True))
        p = e / jnp.sum(e, axis=-1, keepdims=True)
        o_ref[:, sl] = _dot(p.astype(BF16), v_ref[:, sl]).astype(o_ref.dtype)


def _mem_attention(qsrc, q_col_block, kh, vh, tm=512):
    s = qsrc.shape[0]
    mtok, w = kh.shape
    tm = _tile(s, tm)
    blocks = 2 * _nbytes((tm, w), BF16) + 2 * _nbytes((mtok, w), BF16)
    return pl.pallas_call(
        functools.partial(_mem_attn_kernel, heads=MEM_HEADS, scale=(w // MEM_HEADS) ** -0.5),
        out_shape=jax.ShapeDtypeStruct((s, w), BF16),
        grid=(s // tm,),
        in_specs=[pl.BlockSpec((tm, w), lambda i: (i, q_col_block)),
                  pl.BlockSpec((mtok, w), lambda i: (0, 0)),
                  pl.BlockSpec((mtok, w), lambda i: (0, 0))],
        out_specs=pl.BlockSpec((tm, w), lambda i: (i, 0)),
        compiler_params=_params(1, 2 * blocks + 8 * _nbytes((tm, mtok), F32)),
        name="mem_attention",
    )(qsrc, kh, vh)


def _merge_kernel(h_ref, a_sb_ref, a_cv_ref, a_mem_ref, g_sb_ref, g_cv_ref, g_mem_ref,
                  w_sb_ref, w_cv_ref, w_mem_ref, o_ref):
    h = h_ref[...]

    def branch(a_ref, w_ref, g_ref):
        return jax.nn.sigmoid(_dot(h, g_ref[...])) * _dot(a_ref[...], w_ref[...])

    merged = (branch(a_sb_ref, w_sb_ref, g_sb_ref) + branch(a_cv_ref, w_cv_ref, g_cv_ref)
              + branch(a_mem_ref, w_mem_ref, g_mem_ref))
    o_ref[...] = merged.astype(o_ref.dtype)


def _gated_merge(h, a_sb, a_cv, a_mem, w_gates, gate_col0, w_sb, w_cv, w_mem, tm=1024, tn=256):
    m, d = h.shape
    tm, tn = _tile(m, tm), _tile(d, tn)
    assert gate_col0 % tn == 0
    nj, j0 = d // tn, gate_col0 // tn
    row = lambda a: pl.BlockSpec((tm, a.shape[1]), lambda i, j: (i, 0))
    gate = lambda b: pl.BlockSpec((d, tn), lambda i, j: (0, j0 + b * nj + j))
    col = lambda w: pl.BlockSpec((w.shape[0], tn), lambda i, j: (0, j))
    blocks = (_nbytes((tm, d), BF16) + 3 * _nbytes((d, tn), BF16) + _nbytes((tm, tn), BF16)
              + sum(_nbytes((tm, a.shape[1]), BF16) + _nbytes((a.shape[1], tn), BF16)
                    for a in (a_sb, a_cv, a_mem)))
    return pl.pallas_call(
        _merge_kernel,
        out_shape=jax.ShapeDtypeStruct((m, d), BF16),
        grid=(m // tm, nj),
        in_specs=[row(h), row(a_sb), row(a_cv), row(a_mem), gate(0), gate(1), gate(2),
                  col(w_sb), col(w_cv), col(w_mem)],
        out_specs=pl.BlockSpec((tm, tn), lambda i, j: (i, j)),
        compiler_params=_params(2, 2 * blocks + 12 * _nbytes((tm, tn), F32)),
        name="gated_merge",
    )(h, a_sb, a_cv, a_mem, w_gates, w_gates, w_gates, w_sb, w_cv, w_mem)


def _residual_norm_kernel(y_ref, x_ref, g_ref, gn_ref, xo_ref, ho_ref):
    xn = x_ref[...] + _rms(y_ref[...], g_ref[...])
    xo_ref[...] = xn
    ho_ref[...] = _rms(xn, gn_ref[...]).astype(ho_ref.dtype)


def _residual_norm_last_kernel(y_ref, x_ref, g_ref, xo_ref):
    xo_ref[...] = x_ref[...] + _rms(y_ref[...], g_ref[...])


def _residual_norm(y, x, g, g_next, tm=256):
    m, d = x.shape
    tm = _tile(m, tm)
    blk = pl.BlockSpec((tm, d), lambda i: (i, 0))
    vec = pl.BlockSpec((1, d), lambda i: (0, 0))
    f32_blk = _nbytes((tm, d), F32)
    if g_next is None:
        return pl.pallas_call(
            _residual_norm_last_kernel,
            out_shape=jax.ShapeDtypeStruct((m, d), F32),
            grid=(m // tm,), in_specs=[blk, blk, vec], out_specs=blk,
            compiler_params=_params(1, 10 * f32_blk),
            name="residual_norm_last",
        )(y, x, g.reshape(1, d)), None
    return pl.pallas_call(
        _residual_norm_kernel,
        out_shape=(jax.ShapeDtypeStruct((m, d), F32), jax.ShapeDtypeStruct((m, d), BF16)),
        grid=(m // tm,), in_specs=[blk, blk, vec, vec], out_specs=(blk, blk),
        compiler_params=_params(1, 12 * f32_blk),
        name="residual_norm",
    )(y, x, g.reshape(1, d), g_next.reshape(1, d))


def _ffn_up_kernel(halo_ref, h_ref, wg_ref, wv_ref, dwg_ref, dwv_ref, bg_ref, bv_ref, o_ref,
                   a_ref, p_ref, *, tm, halo):
    i = pl.program_id(0)

    @pl.when(pl.program_id(1) == 0)
    def _():
        a_ref[0:halo, :] = jnp.where(i > 0, halo_ref[...], jnp.zeros_like(halo_ref))
        a_ref[halo:halo + tm, :] = h_ref[...]

    a = a_ref[...]

    def conv(w_ref, dw_ref, b_ref):
        p_ref[...] = _dot(a, w_ref[...])
        kw = dw_ref.shape[0]
        u = None
        for t in range(kw):
            off = halo - (kw - 1) + t
            term = dw_ref[t:t + 1, :] * p_ref[off:off + tm, :]
            u = term if u is None else u + term
        return u + b_ref[...]

    u_gate = conv(wg_ref, dwg_ref, bg_ref)
    u_val = conv(wv_ref, dwv_ref, bv_ref)
    o_ref[...] = (jax.nn.gelu(u_gate, approximate=True) * u_val).astype(o_ref.dtype)


def _ffn_up(h, w_up, dw, dwb, tm=1024, tn=512):
    m, d = h.shape
    f = w_up.shape[1] // 2
    kw = dw.shape[0]
    halo = -(-(kw - 1) // BF16_SUBLANES) * BF16_SUBLANES
    tm, tn = _tile(m, tm), _tile(f, tn)
    assert tm % halo == 0
    nj, hb = f // tn, tm // halo
    wspec = lambda half: pl.BlockSpec((d, tn), lambda i, j: (0, j + half * nj))
    dwspec = lambda half: pl.BlockSpec((kw, tn), lambda i, j: (0, j + half * nj))
    bspec = lambda half: pl.BlockSpec((1, tn), lambda i, j: (0, j + half * nj))
    blocks = (_nbytes((halo + tm, d), BF16) + 2 * _nbytes((d, tn), BF16) + _nbytes((tm, tn), BF16))
    scratch = _nbytes((halo + tm, d), BF16) + _nbytes((halo + tm, tn), F32)
    dwb2 = dwb.reshape(1, 2 * f)
    return pl.pallas_call(
        functools.partial(_ffn_up_kernel, tm=tm, halo=halo),
        out_shape=jax.ShapeDtypeStruct((m, f), BF16),
        grid=(m // tm, nj),
        in_specs=[pl.BlockSpec((halo, d), lambda i, j: (jnp.maximum(i * hb - 1, 0), 0)),
                  pl.BlockSpec((tm, d), lambda i, j: (i, 0)),
                  wspec(0), wspec(1), dwspec(0), dwspec(1), bspec(0), bspec(1)],
        out_specs=pl.BlockSpec((tm, tn), lambda i, j: (i, j)),
        scratch_shapes=[pltpu.VMEM((halo + tm, d), BF16), pltpu.VMEM((halo + tm, tn), F32)],
        compiler_params=_params(2, 2 * blocks + scratch + 6 * _nbytes((tm, tn), F32)),
        name="ffn_up",
    )(h, h, w_up, w_up, dw, dw, dwb2, dwb2)


def _forward_one(x, mem, p):
    depth = p["w_in"].shape[0]
    d = x.shape[1]
    sbw = p["sb_wo"].shape[1]
    ch = p["cv_wo"].shape[1]
    mw = p["mem_wo"].shape[1]
    heads = sbw // SB_HEAD_DIM
    assert (3 * sbw) % mw == 0
    glu0, mq0, gate0 = 3 * sbw, 3 * sbw + 2 * ch, 3 * sbw + 2 * ch + mw

    h = _rmsnorm(x, p["norm_mix_pre"][0])
    for l in range(depth):
        w_in = _layer_bf16(p["w_in"], l)

        qkvm = _matmul(h, w_in, BF16, "proj_qkvm", [(0, glu0), (mq0, mw)])
        c = _glu_proj(h, w_in, glu0, ch)

        a_sb = _sb_attention(qkvm, heads)
        a_cv = _conv_module(c, p["cv_dw"][l], p["cv_dw_b"][l], p["cv_ln_g"][l], p["cv_ln_b"][l])

        mem_n = _rmsnorm(mem, p["mem_norm"][l])
        kh = _matmul(mem_n, _layer_bf16(p["mem_wk"], l), BF16, "mem_k")
        vh = _matmul(mem_n, _layer_bf16(p["mem_wv"], l), BF16, "mem_v")
        a_mem = _mem_attention(qkvm, (3 * sbw) // mw, kh, vh)

        merged = _gated_merge(h, a_sb, a_cv, a_mem, w_in, gate0, _layer_bf16(p["sb_wo"], l),
                              _layer_bf16(p["cv_wo"], l), _layer_bf16(p["mem_wo"], l))
        y = _matmul(merged, _layer_bf16(p["w_out"], l), F32, "proj_out")
        x, h2 = _residual_norm(y, x, p["norm_mix_post"][l], p["norm_ffn_pre"][l])

        f = _ffn_up(h2, _layer_bf16(p["ffn_up"], l), p["ffn_dw"][l], p["ffn_dw_b"][l])
        y = _matmul(f, _layer_bf16(p["ffn_down"], l), F32, "ffn_down")
        g_next = p["norm_mix_pre"][l + 1] if l + 1 < depth else None
        x, h = _residual_norm(y, x, p["norm_ffn_post"][l], g_next)
    return x


def kernel(x, mem, norm_mix_pre, w_in, sb_wo, cv_dw, cv_dw_b, cv_ln_g, cv_ln_b, cv_wo, mem_norm, mem_wk, mem_wv, mem_wo, w_out, norm_mix_post, norm_ffn_pre, ffn_up, ffn_dw, ffn_dw_b, ffn_down, norm_ffn_post):
    p = dict(norm_mix_pre=norm_mix_pre, w_in=w_in, sb_wo=sb_wo, cv_dw=cv_dw, cv_dw_b=cv_dw_b,
             cv_ln_g=cv_ln_g, cv_ln_b=cv_ln_b, cv_wo=cv_wo, mem_norm=mem_norm, mem_wk=mem_wk,
             mem_wv=mem_wv, mem_wo=mem_wo, w_out=w_out, norm_mix_post=norm_mix_post,
             norm_ffn_pre=norm_ffn_pre, ffn_up=ffn_up, ffn_dw=ffn_dw, ffn_dw_b=ffn_dw_b,
             ffn_down=ffn_down, norm_ffn_post=norm_ffn_post)
    return jnp.stack([_forward_one(x[b], mem[b], p) for b in range(x.shape[0])])
```
